```python
import jax, jax.numpy as jnp
from jax import lax
import numpy as np

D_MODEL = 2048
BATCH = 16
SEQ = 2048
DEPTH = 4

D_MIX = D_MODEL
HG_HEADS = 8
HG_KEY_DIM = 128
HG_VAL_DIM = 128
HG_QK = HG_HEADS * HG_KEY_DIM
HG_WIDTH = HG_HEADS * HG_VAL_DIM
HG_CHUNK = 32
MLA_HEADS = 8
MLA_NOPE = 128
MLA_ROPE = 64
MLA_V = 128
MLA_Q_RANK = 512
MLA_KV_RANK = 256
MLA_WIDTH = MLA_HEADS * MLA_V
Q_BLOCK = 128
ROPE_THETA = 10000.0
IN_SIZES = (HG_QK, HG_QK, HG_WIDTH, HG_WIDTH, MLA_Q_RANK, MLA_KV_RANK, MLA_ROPE)
D_IN = HG_QK * 2 + HG_WIDTH * 2 + MLA_Q_RANK + MLA_KV_RANK + MLA_ROPE
D_FF = 5632
N_SUB = 3
EPS = 1e-6

kernel_name = "hymba_hgrn2_mla_macaron_adaln"


def rmsnorm(x, w):
    xf = x.astype(jnp.float32)
    y = xf * lax.rsqrt(jnp.mean(xf * xf, axis=-1, keepdims=True) + EPS)
    return (y * w.astype(jnp.float32)).astype(x.dtype)


def modulate(x, g, shift, scale):
    return rmsnorm(x, g) * (1 + scale[:, None, :]) + shift[:, None, :]


def swiglu(h, w_gate, w_up, w_down):
    return (jax.nn.silu(h @ w_gate) * (h @ w_up)) @ w_down


def apply_rope(x, cos, sin):
    x1, x2 = jnp.split(x, 2, axis=-1)
    xf1, xf2 = x1.astype(jnp.float32), x2.astype(jnp.float32)
    return jnp.concatenate([xf1 * cos - xf2 * sin, xf1 * sin + xf2 * cos], axis=-1).astype(x.dtype)


def hgrn2_chunkwise(q, k, v, log_f):
    B, S, H, DK = q.shape
    DV = v.shape[-1]
    n_chunks = S // HG_CHUNK

    def to_chunks(t):
        return t.astype(jnp.float32).reshape(B, n_chunks, HG_CHUNK, H, t.shape[-1]).transpose(1, 0, 3, 2, 4)

    q, k, v, log_f = to_chunks(q), to_chunks(k), to_chunks(v), to_chunks(log_f)
    b = jnp.cumsum(log_f, axis=-2)
    b_end = b[..., -1:, :]
    q_in = q * jnp.exp(b)
    k_in = k * jnp.exp(-b)
    k_out = k * jnp.exp(b_end - b)
    decay_chunk = jnp.exp(b_end[..., 0, :])
    causal = jnp.tril(jnp.ones((HG_CHUNK, HG_CHUNK), dtype=bool))
    a = jnp.einsum('nbhtd,nbhsd->nbhts', q_in, k_in)
    a = jnp.where(causal, a, 0.0)
    o_intra = jnp.einsum('nbhts,nbhsv->nbhtv', a, v)

    def step(state, xs):
        qi, ko, vc, dc = xs
        o_inter = jnp.einsum('bhtd,bhdv->bhtv', qi, state)
        state = state * dc[..., None] + jnp.einsum('bhsd,bhsv->bhdv', ko, vc)
        return state, o_inter

    state0 = jnp.zeros((B, H, DK, DV), jnp.float32)
    _, o_inter = lax.scan(step, state0, (q_in, k_out, v, decay_chunk))
    o = o_intra + o_inter
    return o.transpose(1, 0, 3, 2, 4).reshape(B, S, H, DV)


def causal_block_attention(q, k, v):
    S = q.shape[1]
    scale = (MLA_NOPE + MLA_ROPE) ** -0.5
    outs = []
    for blk in range(S // Q_BLOCK):
        lo, hi = blk * Q_BLOCK, (blk + 1) * Q_BLOCK
        qb, kb, vb = q[:, lo:hi], k[:, :hi], v[:, :hi]
        s = jnp.einsum('bqhd,bkhd->bhqk', qb, kb).astype(jnp.float32) * scale
        mask = jnp.arange(hi)[None, :] <= (lo + jnp.arange(Q_BLOCK))[:, None]
        s = jnp.where(mask, s, -jnp.inf)
        p = jax.nn.softmax(s, axis=-1).astype(vb.dtype)
        outs.append(jnp.einsum('bhqk,bkhv->bqhv', p, vb))
    return jnp.concatenate(outs, axis=1)


def token_mix(h, cos, sin, lb, w_in, qa_norm_g, w_q_up, kva_norm_g, w_kv_up, hg_norm_g, w_out):
    B, S, _ = h.shape
    proj = h @ w_in
    split_points = []
    acc = 0
    for size in IN_SIZES[:-1]:
        acc += size
        split_points.append(acc)
    hq, hf, hi, hg, qa, kva, kpe = jnp.split(proj, split_points, axis=-1)

    q = jax.nn.silu(hq).reshape(B, S, HG_HEADS, HG_KEY_DIM)
    z = hf.astype(jnp.float32).reshape(B, S, HG_HEADS, HG_KEY_DIM)
    lb = lb.reshape(HG_HEADS, HG_KEY_DIM)
    log_f = jnp.logaddexp(jnp.log(lb), jnp.log1p(-lb) + jax.nn.log_sigmoid(z))
    k = (1.0 - lb) * jax.nn.sigmoid(-z)
    v = hi.reshape(B, S, HG_HEADS, HG_VAL_DIM)
    o_hg = hgrn2_chunkwise(q, k, v, log_f).astype(h.dtype)
    o_hg = rmsnorm(o_hg, hg_norm_g) * jax.nn.silu(hg.reshape(B, S, HG_HEADS, HG_VAL_DIM))
    o_hg = o_hg.reshape(B, S, HG_WIDTH)

    cq = rmsnorm(qa, qa_norm_g)
    qh = (cq @ w_q_up).reshape(B, S, MLA_HEADS, MLA_NOPE + MLA_ROPE)
    q_nope, q_pe = qh[..., :MLA_NOPE], qh[..., MLA_NOPE:]
    ckv = rmsnorm(kva, kva_norm_g)
    kvh = (ckv @ w_kv_up).reshape(B, S, MLA_HEADS, MLA_NOPE + MLA_V)
    k_nope, v_m = kvh[..., :MLA_NOPE], kvh[..., MLA_NOPE:]
    q_pe = apply_rope(q_pe, cos, sin)
    k_pe = apply_rope(kpe[:, :, None, :], cos, sin)
    q_full = jnp.concatenate([q_nope, q_pe], axis=-1)
    k_full = jnp.concatenate([k_nope, jnp.broadcast_to(k_pe, (B, S, MLA_HEADS, MLA_ROPE))], axis=-1)
    o_mla = causal_block_attention(q_full, k_full, v_m).reshape(B, S, MLA_WIDTH)

    return jnp.concatenate([o_hg, o_mla], axis=-1) @ w_out


def setup_inputs(seed: int = 0) -> dict:
    key = jax.random.key(seed)
    ks = jax.random.split(key, 20)
    f32 = jnp.float32

    def nrm(k, shape, fan_in, mult=1.0):
        return jax.random.normal(k, shape, f32) * (mult * fan_in ** -0.5)

    x = jax.random.normal(ks[0], (BATCH, SEQ, D_MODEL), f32)
    c = jax.random.normal(ks[1], (BATCH, D_MODEL), f32)
    offsets = jax.random.randint(ks[2], (BATCH, 1), 0, 4096, dtype=jnp.int32)
    positions = (offsets + jnp.arange(SEQ, dtype=jnp.int32)[None, :]).astype(jnp.int32)
    w_ada = nrm(ks[3], (DEPTH, D_MODEL, N_SUB * 3 * D_MODEL), D_MODEL, 0.5)
    b_ada = 0.02 * jax.random.normal(ks[4], (DEPTH, N_SUB * 3 * D_MODEL), f32)
    norm_g = 1.0 + 0.02 * jax.random.normal(ks[5], (DEPTH, N_SUB, D_MODEL), f32)
    w_in = nrm(ks[6], (DEPTH, D_MODEL, D_IN), D_MODEL)
    qa_norm_g = 1.0 + 0.02 * jax.random.normal(ks[7], (DEPTH, MLA_Q_RANK), f32)
    w_q_up = nrm(ks[8], (DEPTH, MLA_Q_RANK, MLA_HEADS * (MLA_NOPE + MLA_ROPE)), MLA_Q_RANK)
    kva_norm_g = 1.0 + 0.02 * jax.random.normal(ks[9], (DEPTH, MLA_KV_RANK), f32)
    w_kv_up = nrm(ks[10], (DEPTH, MLA_KV_RANK, MLA_HEADS * (MLA_NOPE + MLA_V)), MLA_KV_RANK)
    hg_lb_logits = jax.random.normal(ks[11], (DEPTH, HG_QK), f32)
    hg_norm_g = 1.0 + 0.02 * jax.random.normal(ks[12], (DEPTH, HG_VAL_DIM), f32)
    w_out = nrm(ks[13], (DEPTH, D_MIX, D_MODEL), D_MIX)
    ffn_w_gate = nrm(ks[14], (DEPTH, 2, D_MODEL, D_FF), D_MODEL)
    ffn_w_up = nrm(ks[15], (DEPTH, 2, D_MODEL, D_FF), D_MODEL)
    ffn_w_down = nrm(ks[16], (DEPTH, 2, D_FF, D_MODEL), D_FF)
    final_norm_g = 1.0 + 0.02 * jax.random.normal(ks[17], (D_MODEL,), f32)
    return {"x": x, "c": c, "positions": positions, "w_ada": w_ada, "b_ada": b_ada,
            "norm_g": norm_g, "w_in": w_in, "qa_norm_g": qa_norm_g, "w_q_up": w_q_up,
            "kva_norm_g": kva_norm_g, "w_kv_up": w_kv_up, "hg_lb_logits": hg_lb_logits,
            "hg_norm_g": hg_norm_g, "w_out": w_out, "ffn_w_gate": ffn_w_gate,
            "ffn_w_up": ffn_w_up, "ffn_w_down": ffn_w_down, "final_norm_g": final_norm_g}


def reference(x, c, positions, w_ada, b_ada, norm_g, w_in, qa_norm_g, w_q_up, kva_norm_g,
              w_kv_up, hg_lb_logits, hg_norm_g, w_out, ffn_w_gate, ffn_w_up, ffn_w_down,
              final_norm_g):
    B = x.shape[0]
    half = MLA_ROPE // 2
    inv_freq = ROPE_THETA ** (-jnp.arange(half, dtype=jnp.float32) / half)
    ang = positions.astype(jnp.float32)[..., None] * inv_freq
    cos = jnp.cos(ang)[:, :, None, :]
    sin = jnp.sin(ang)[:, :, None, :]
    lb_all = jnp.cumsum(jax.nn.softmax(hg_lb_logits.astype(jnp.float32), axis=0), axis=0)
    lb_all = lb_all - lb_all[0:1]
    c_act = jax.nn.silu(c)

    for l in range(DEPTH):
        mod = (c_act @ w_ada[l] + b_ada[l]).reshape(B, N_SUB, 3, D_MODEL)
        h = modulate(x, norm_g[l, 0], mod[:, 0, 0], mod[:, 0, 1])
        x = x + 0.5 * mod[:, 0, 2][:, None, :] * swiglu(h, ffn_w_gate[l, 0], ffn_w_up[l, 0], ffn_w_down[l, 0])
        h = modulate(x, norm_g[l, 1], mod[:, 1, 0], mod[:, 1, 1])
        y = token_mix(h, cos, sin, lb_all[l], w_in[l], qa_norm_g[l], w_q_up[l], kva_norm_g[l],
                      w_kv_up[l], hg_norm_g[l], w_out[l])
        x = x + mod[:, 1, 2][:, None, :] * y
        h = modulate(x, norm_g[l, 2], mod[:, 2, 0], mod[:, 2, 1])
        x = x + 0.5 * mod[:, 2, 2][:, None, :] * swiglu(h, ffn_w_gate[l, 1], ffn_w_up[l, 1], ffn_w_down[l, 1])

    return rmsnorm(x, final_norm_g)
```

```python
import functools

import jax
import jax.numpy as jnp
from jax import lax
from jax.experimental import pallas as pl
from jax.experimental.pallas import tpu as pltpu

EPS = 1e-6
N_SUB = 3
HG_HEADS = 8
HG_DIM = 128
HG_CHUNK = 32
MLA_HEADS = 8
MLA_NOPE = 128
MLA_ROPE = 64
MLA_V = 128
MLA_Q_RANK = 512
MLA_KV_RANK = 256
ROPE_THETA = 10000.0
MLA_QK_PAD = 256

V7X_LANES = 128
V7X_VMEM_BYTES = 64 * 1024 * 1024
VMEM_LIMIT = V7X_VMEM_BYTES - 8 * 1024 * 1024

F32 = jnp.float32
BF16 = jnp.bfloat16


def _params(*sem):
    return pltpu.CompilerParams(dimension_semantics=sem, vmem_limit_bytes=VMEM_LIMIT)


def _sigmoid(x):
    return 1.0 / (1.0 + jnp.exp(-x))


def _silu(x):
    return x * _sigmoid(x)


def _rms(x, g):
    ms = jnp.mean(x * x, axis=-1, keepdims=True)
    return (x * lax.rsqrt(ms + EPS)) * g


def _modulated_norm(x, g, shift, scale):
    return _rms(x, g) * (1.0 + scale) + shift


def _dot(a, b):
    return jnp.dot(a, b, preferred_element_type=F32)


def _dot_nt(a, b):
    return lax.dot_general(a, b, (((1,), (1,)), ((), ())), preferred_element_type=F32)


def _dot_tn(a, b):
    return lax.dot_general(a, b, (((0,), (0,)), ((), ())), preferred_element_type=F32)


def _tile(n, pref):
    t = min(n, pref)
    assert n % t == 0, (n, t)
    return t


def _ada_kernel(c_ref, w_ref, b_ref, o_ref):
    ca = _silu(c_ref[...]).astype(BF16)
    o_ref[...] = _dot(ca, w_ref[...].astype(BF16)) + b_ref[...]


def _ada(c, w_ada, b_ada):
    L, D, N = w_ada.shape
    B = c.shape[0]
    tn = _tile(N, 1024)
    return pl.pallas_call(
        _ada_kernel,
        grid=(L, N // tn),
        in_specs=[
            pl.BlockSpec((B, D), lambda l, j: (0, 0)),
            pl.BlockSpec((None, D, tn), lambda l, j: (l, 0, j)),
            pl.BlockSpec((None, 1, tn), lambda l, j: (l, 0, j)),
        ],
        out_specs=pl.BlockSpec((None, B, tn), lambda l, j: (l, 0, j)),
        out_shape=jax.ShapeDtypeStruct((L, B, N), F32),
        compiler_params=_params("parallel", "parallel"),
    )(c, w_ada, b_ada.reshape(L, 1, N))


def _ffn_kernel(x_ref, mod_ref, g_ref, wg_ref, wu_ref, wd_ref, o_ref, h_ref, *, sub):
    j = pl.program_id(1)

    @pl.when(j == 0)
    def _():
        h = _modulated_norm(x_ref[...], g_ref[...], mod_ref[3 * sub:3 * sub + 1, :],
                            mod_ref[3 * sub + 1:3 * sub + 2, :])
        h_ref[...] = h.astype(BF16)
        o_ref[...] = jnp.zeros_like(o_ref)

    h = h_ref[...]
    gate = _dot(h, wg_ref[...])
    up = _dot(h, wu_ref[...])
    a = (_silu(gate) * up).astype(BF16)
    o_ref[...] += _dot(a, wd_ref[...])

    @pl.when(j == pl.num_programs(1) - 1)
    def _():
        o_ref[...] = x_ref[...] + (0.5 * mod_ref[3 * sub + 2:3 * sub + 3, :]) * o_ref[...]


def _ffn(x, mod_l, g, wg, wu, wd, layer, which, sub, seq):
    T, D = x.shape
    F = wg.shape[-1]
    tm = _tile(seq, 512)
    tf = _tile(F, 512)
    per_seq = seq // tm
    return pl.pallas_call(
        functools.partial(_ffn_kernel, sub=sub),
        grid=(T // tm, F // tf),
        in_specs=[
            pl.BlockSpec((tm, D), lambda i, j: (i, 0)),
            pl.BlockSpec((None, 3 * N_SUB, D), lambda i, j: (i // per_seq, 0, 0)),
            pl.BlockSpec((1, D), lambda i, j: (0, 0)),
            pl.BlockSpec((None, None, D, tf), lambda i, j: (layer, which, 0, j)),
            pl.BlockSpec((None, None, D, tf), lambda i, j: (layer, which, 0, j)),
            pl.BlockSpec((None, None, tf, D), lambda i, j: (layer, which, j, 0)),
        ],
        out_specs=pl.BlockSpec((tm, D), lambda i, j: (i, 0)),
        out_shape=jax.ShapeDtypeStruct((T, D), F32),
        scratch_shapes=[pltpu.VMEM((tm, D), BF16)],
        compiler_params=_params("parallel", "arbitrary"),
    )(x, mod_l, g, wg, wu, wd)


def _inproj_kernel(x_ref, mod_ref, g_ref, w_ref, o_ref, h_ref, *, sub):
    @pl.when(pl.program_id(1) == 0)
    def _():
        h = _modulated_norm(x_ref[...], g_ref[...], mod_ref[3 * sub:3 * sub + 1, :],
                            mod_ref[3 * sub + 1:3 * sub + 2, :])
        h_ref[...] = h.astype(BF16)

    o_ref[...] = _dot(h_ref[...], w_ref[...])


def _inproj(x, mod_l, g, w, seq):
    T, D = x.shape
    N = w.shape[-1]
    tm = _tile(seq, 512)
    tn = N // 3
    assert N % 3 == 0 and tn % V7X_LANES == 0
    per_seq = seq // tm
    return pl.pallas_call(
        functools.partial(_inproj_kernel, sub=1),
        grid=(T // tm, N // tn),
        in_specs=[
            pl.BlockSpec((tm, D), lambda i, j: (i, 0)),
            pl.BlockSpec((None, 3 * N_SUB, D), lambda i, j: (i // per_seq, 0, 0)),
            pl.BlockSpec((1, D), lambda i, j: (0, 0)),
            pl.BlockSpec((D, tn), lambda i, j: (0, j)),
        ],
        out_specs=pl.BlockSpec((tm, tn), lambda i, j: (i, j)),
        out_shape=jax.ShapeDtypeStruct((T, N), F32),
        scratch_shapes=[pltpu.VMEM((tm, D), BF16)],
        compiler_params=_params("parallel", "arbitrary"),
    )(x, mod_l, g, w)


def _hgrn_kernel(hq_ref, hf_ref, hi_ref, hg_ref, lb_ref, gn_ref, o_ref, st_ref, *, ts):
    C = HG_CHUNK
    nc = ts // C

    @pl.when(pl.program_id(2) == 0)
    def _():
        st_ref[...] = jnp.zeros_like(st_ref)

    z = hf_ref[...]
    lb = lb_ref[...]
    log_sig = jnp.minimum(z, 0.0) - jnp.log1p(jnp.exp(-jnp.abs(z)))
    log_f = jnp.logaddexp(jnp.log(lb), jnp.log1p(-lb) + log_sig)
    k = (1.0 - lb) * _sigmoid(-z)
    q = _silu(hq_ref[...])

    pos = lax.broadcasted_iota(jnp.int32, (ts, HG_DIM), 0) % C
    b = log_f
    step = 1
    while step < C:
        b = b + jnp.where(pos >= step, pltpu.roll(b, step, axis=0), 0.0)
        step *= 2

    b3 = b.reshape(nc, C, HG_DIM)
    b_end = b3[:, C - 1:C, :]
    q3 = q.reshape(nc, C, HG_DIM)
    k3 = k.reshape(nc, C, HG_DIM)
    q_in = (q3 * jnp.exp(b3)).astype(BF16)
    k_in = (k3 * jnp.exp(-b3)).astype(BF16)
    k_out = (k3 * jnp.exp(b_end - b3)).astype(BF16)
    decay = jnp.exp(b_end)
    v = hi_ref[...].astype(BF16).reshape(nc, C, HG_DIM)

    causal = (lax.broadcasted_iota(jnp.int32, (C, C), 1)
              <= lax.broadcasted_iota(jnp.int32, (C, C), 0))
    st = st_ref[...]
    outs = []
    for c in range(nc):
        a = jnp.where(causal, _dot_nt(q_in[c], k_in[c]), 0.0)
        o = _dot(a.astype(BF16), v[c]) + _dot_nt(q_in[c], st.astype(BF16))
        st = st * decay[c] + _dot_tn(v[c], k_out[c])
        outs.append(o)
    st_ref[...] = st

    o = jnp.concatenate(outs, axis=0)
    o_ref[...] = (_rms(o, gn_ref[...]) * _silu(hg_ref[...])).astype(BF16)


def _hgrn(proj, lb, gn, batch, seq):
    T = proj.shape[0]
    ts = _tile(seq, 512)
    ns = seq // ts
    H = HG_HEADS

    def col(group):
        return pl.BlockSpec((ts, HG_DIM), lambda b, h, s: (b * ns + s, group * H + h))

    return pl.pallas_call(
        functools.partial(_hgrn_kernel, ts=ts),
        grid=(batch, H, ns),
        in_specs=[col(0), col(1), col(2), col(3),
                  pl.BlockSpec((1, HG_DIM), lambda b, h, s: (0, h)),
                  pl.BlockSpec((1, HG_DIM), lambda b, h, s: (0, 0))],
        out_specs=pl.BlockSpec((ts, HG_DIM), lambda b, h, s: (b * ns + s, h)),
        out_shape=jax.ShapeDtypeStruct((T, H * HG_DIM), BF16),
        scratch_shapes=[pltpu.VMEM((HG_DIM, HG_DIM), F32)],
        compiler_params=_params("parallel", "parallel", "arbitrary"),
    )(proj, proj, proj, proj, lb, gn)


def _mla_proj_kernel(qa_ref, kva_ref, kpe_ref, tab_ref, qg_ref, kg_ref, wq_ref, wk_ref, wv_ref,
                     qf_ref, kf_ref, v_ref, *, tm):
    scale = (MLA_NOPE + MLA_ROPE) ** -0.5
    tab = tab_ref[...]
    low = lax.broadcasted_iota(jnp.int32, (tm, V7X_LANES), 1) < MLA_ROPE

    def rope(y2):
        y = y2 * tab
        return jnp.where(low, y + pltpu.roll(y, MLA_ROPE, axis=1), 0.0)

    cq = _rms(qa_ref[...], qg_ref[...]).astype(BF16)
    qf = _dot(cq, wq_ref[...])
    ckv = _rms(kva_ref[...], kg_ref[...]).astype(BF16)
    kn = _dot(ckv, wk_ref[...])
    v_ref[...] = _dot(ckv, wv_ref[...]).astype(BF16)
    kr = rope(kpe_ref[...]).astype(BF16)
    P = MLA_QK_PAD
    for h in range(MLA_HEADS):
        qf_ref[:, h * P:h * P + MLA_NOPE] = (qf[:, h * P:h * P + MLA_NOPE] * scale).astype(BF16)
        qf_ref[:, h * P + MLA_NOPE:(h + 1) * P] = (rope(qf[:, h * P + MLA_NOPE:(h + 1) * P]) * scale).astype(BF16)
        kf_ref[:, h * P:h * P + MLA_NOPE] = kn[:, h * MLA_NOPE:(h + 1) * MLA_NOPE].astype(BF16)
        kf_ref[:, h * P + MLA_NOPE:(h + 1) * P] = kr


def _mla_proj(proj, tab, qg, kg, wq, wk, wv, seq):
    T = proj.shape[0]
    tm = _tile(seq, 512)
    qa_col = (4 * HG_HEADS * HG_DIM) // MLA_Q_RANK
    kva_col = (4 * HG_HEADS * HG_DIM + MLA_Q_RANK) // MLA_KV_RANK
    kpe_col = (4 * HG_HEADS * HG_DIM + MLA_Q_RANK + MLA_KV_RANK) // V7X_LANES
    full = lambda a: pl.BlockSpec(a.shape, lambda i: (0, 0))
    H = MLA_HEADS
    return pl.pallas_call(
        functools.partial(_mla_proj_kernel, tm=tm),
        grid=(T // tm,),
        in_specs=[
            pl.BlockSpec((tm, MLA_Q_RANK), lambda i: (i, qa_col)),
            pl.BlockSpec((tm, MLA_KV_RANK), lambda i: (i, kva_col)),
            pl.BlockSpec((tm, V7X_LANES), lambda i: (i, kpe_col)),
            pl.BlockSpec((tm, V7X_LANES), lambda i: (i, 0)),
            full(qg), full(kg), full(wq), full(wk), full(wv),
        ],
        out_specs=[
            pl.BlockSpec((tm, H * MLA_QK_PAD), lambda i: (i, 0)),
            pl.BlockSpec((tm, H * MLA_QK_PAD), lambda i: (i, 0)),
            pl.BlockSpec((tm, H * MLA_V), lambda i: (i, 0)),
        ],
        out_shape=[
            jax.ShapeDtypeStruct((T, H * MLA_QK_PAD), BF16),
            jax.ShapeDtypeStruct((T, H * MLA_QK_PAD), BF16),
            jax.ShapeDtypeStruct((T, H * MLA_V), BF16),
        ],
        compiler_params=_params("parallel"),
    )(proj, proj, proj, tab, qg, kg, wq, wk, wv)


def _attn_kernel(q_ref, k_ref, v_ref, o_ref, m_ref, l_ref, acc_ref, *, t):
    i = pl.program_id(2)
    j = pl.program_id(3)

    @pl.when(j == 0)
    def _():
        m_ref[...] = jnp.full_like(m_ref, -jnp.inf)
        l_ref[...] = jnp.zeros_like(l_ref)
        acc_ref[...] = jnp.zeros_like(acc_ref)

    def update(masked):
        s = _dot_nt(q_ref[...], k_ref[...])
        if masked:
            keep = (lax.broadcasted_iota(jnp.int32, (t, t), 1)
                    <= lax.broadcasted_iota(jnp.int32, (t, t), 0))
            s = jnp.where(keep, s, -jnp.inf)
        m_prev = m_ref[...]
        m_new = jnp.maximum(m_prev, jnp.max(s, axis=-1, keepdims=True))
        alpha = jnp.exp(m_prev - m_new)
        p = jnp.exp(s - m_new)
        l_ref[...] = alpha * l_ref[...] + jnp.sum(p, axis=-1, keepdims=True)
        acc_ref[...] = alpha * acc_ref[...] + _dot(p.astype(BF16), v_ref[...])
        m_ref[...] = m_new

    @pl.when(j < i)
    def _():
        update(False)

    @pl.when(j == i)
    def _():
        update(True)
        o_ref[...] = (acc_ref[...] / l_ref[...]).astype(BF16)


def _attn(qf, kf, v, batch, seq):
    T = qf.shape[0]
    t = _tile(seq, 512)
    n = seq // t
    H = MLA_HEADS
    return pl.pallas_call(
        functools.partial(_attn_kernel, t=t),
        grid=(batch, H, n, n),
        in_specs=[
            pl.BlockSpec((t, MLA_QK_PAD), lambda b, h, i, j: (b * n + i, h)),
            pl.BlockSpec((t, MLA_QK_PAD), lambda b, h, i, j: (b * n + jnp.minimum(i, j), h)),
            pl.BlockSpec((t, MLA_V), lambda b, h, i, j: (b * n + jnp.minimum(i, j), h)),
        ],
        out_specs=pl.BlockSpec((t, MLA_V), lambda b, h, i, j: (b * n + i, h)),
        out_shape=jax.ShapeDtypeStruct((T, H * MLA_V), BF16),
        scratch_shapes=[pltpu.VMEM((t, 1), F32), pltpu.VMEM((t, 1), F32), pltpu.VMEM((t, MLA_V), F32)],
        compiler_params=_params("parallel", "parallel", "parallel", "arbitrary"),
    )(qf, kf, v)


def _outproj_kernel(x_ref, mod_ref, a_ref, b_ref, wa_ref, wb_ref, o_ref, *, sub):
    y = _dot(a_ref[...], wa_ref[...]) + _dot(b_ref[...], wb_ref[...])
    o_ref[...] = x_ref[...] + mod_ref[3 * sub + 2:3 * sub + 3, :] * y


def _outproj(x, mod_l, a, b, wa, wb, seq):
    T, D = x.shape
    tm = _tile(seq, 512)
    per_seq = seq // tm
    return pl.pallas_call(
        functools.partial(_outproj_kernel, sub=1),
        grid=(T // tm,),
        in_specs=[
            pl.BlockSpec((tm, D), lambda i: (i, 0)),
            pl.BlockSpec((None, 3 * N_SUB, D), lambda i: (i // per_seq, 0, 0)),
            pl.BlockSpec((tm, a.shape[1]), lambda i: (i, 0)),
            pl.BlockSpec((tm, b.shape[1]), lambda i: (i, 0)),
            pl.BlockSpec(wa.shape, lambda i: (0, 0)),
            pl.BlockSpec(wb.shape, lambda i: (0, 0)),
        ],
        out_specs=pl.BlockSpec((tm, D), lambda i: (i, 0)),
        out_shape=jax.ShapeDtypeStruct((T, D), F32),
        compiler_params=_params("parallel"),
    )(x, mod_l, a, b, wa, wb)


def _final_kernel(x_ref, g_ref, o_ref):
    o_ref[...] = _rms(x_ref[...], g_ref[...])


def _final_norm(x, g, seq):
    T, D = x.shape
    tm = _tile(seq, 512)
    return pl.pallas_call(
        _final_kernel,
        grid=(T // tm,),
        in_specs=[pl.BlockSpec((tm, D), lambda i: (i, 0)), pl.BlockSpec((1, D), lambda i: (0, 0))],
        out_specs=pl.BlockSpec((tm, D), lambda i: (i, 0)),
        out_shape=jax.ShapeDtypeStruct((T, D), F32),
        compiler_params=_params("parallel"),
    )(x, g)


def _rotate_half_cols(w):
    half = MLA_ROPE // 2
    w1, w2 = w[..., :half], w[..., half:]
    return jnp.concatenate([-w2, w1], axis=-1)


def _prep_w_in(w_in):
    kpe = w_in[..., -MLA_ROPE:]
    return jnp.concatenate([w_in, _rotate_half_cols(kpe)], axis=-1).astype(BF16)


def _prep_w_q(w_q_up):
    L, R, _ = w_q_up.shape
    w = w_q_up.reshape(L, R, MLA_HEADS, MLA_NOPE + MLA_ROPE)
    pe = w[..., MLA_NOPE:]
    w = jnp.concatenate([w, _rotate_half_cols(pe)], axis=-1)
    return w.reshape(L, R, MLA_HEADS * MLA_QK_PAD).astype(BF16)


def _prep_w_kv(w_kv_up):
    L, R, _ = w_kv_up.shape
    w = w_kv_up.reshape(L, R, MLA_HEADS, MLA_NOPE + MLA_V)
    wk = w[..., :MLA_NOPE].reshape(L, R, MLA_HEADS * MLA_NOPE)
    wv = w[..., MLA_NOPE:].reshape(L, R, MLA_HEADS * MLA_V)
    return wk.astype(BF16), wv.astype(BF16)


def kernel(x, c, positions, w_ada, b_ada, norm_g, w_in, qa_norm_g, w_q_up, kva_norm_g, w_kv_up,
           hg_lb_logits, hg_norm_g, w_out, ffn_w_gate, ffn_w_up, ffn_w_down, final_norm_g):
    B, S, D = x.shape
    L = w_ada.shape[0]
    T = B * S

    half = MLA_ROPE // 2
    inv_freq = ROPE_THETA ** (-jnp.arange(half, dtype=F32) / half)
    ang = positions.astype(F32).reshape(T, 1) * inv_freq
    cos, sin = jnp.cos(ang), jnp.sin(ang)
    tab = jnp.concatenate([cos, cos, sin, sin], axis=-1)

    lb_all = jnp.cumsum(jax.nn.softmax(hg_lb_logits.astype(F32), axis=0), axis=0)
    lb_all = lb_all - lb_all[0:1]

    w_in_p = _prep_w_in(w_in)
    w_q_p = _prep_w_q(w_q_up)
    w_k_p, w_v_p = _prep_w_kv(w_kv_up)
    hg_w = HG_HEADS * HG_DIM
    w_out_a = w_out[:, :hg_w, :].astype(BF16)
    w_out_b = w_out[:, hg_w:, :].astype(BF16)
    wg, wu, wd = ffn_w_gate.astype(BF16), ffn_w_up.astype(BF16), ffn_w_down.astype(BF16)

    mod = _ada(c, w_ada, b_ada).reshape(L, B, 3 * N_SUB, D)

    xf = x.reshape(T, D)
    for l in range(L):
        mod_l = mod[l]
        xf = _ffn(xf, mod_l, norm_g[l, 0:1], wg, wu, wd, l, 0, 0, S)
        proj = _inproj(xf, mod_l, norm_g[l, 1:2], w_in_p[l], S)
        o_hg = _hgrn(proj, lb_all[l:l + 1], hg_norm_g[l:l + 1], B, S)
        qf, kf, v = _mla_proj(proj, tab, qa_norm_g[l:l + 1], kva_norm_g[l:l + 1],
                              w_q_p[l], w_k_p[l], w_v_p[l], S)
        o_mla = _attn(qf, kf, v, B, S)
        xf = _outproj(xf, mod_l, o_hg, o_mla, w_out_a[l], w_out_b[l], S)
        xf = _ffn(xf, mod_l, norm_g[l, 2:3], wg, wu, wd, l, 1, 2, S)
    return _final_norm(xf, final_norm_g.reshape(1, D), S).reshape(B, S, D)
```

```python
import functools

import jax
import jax.numpy as jnp
from jax import lax
from jax.experimental import pallas as pl
from jax.experimental.pallas import tpu as pltpu

EPS = 1e-6
N_SUB = 3
HG_HEADS = 8
HG_DIM = 128
HG_CHUNK = 32
HG_GROUP = 4
MLA_HEADS = 8
MLA_NOPE = 128
MLA_ROPE = 64
MLA_V = 128
MLA_Q_RANK = 512
MLA_KV_RANK = 256
ROPE_THETA = 10000.0
MLA_QK_PAD = 256

V7X_LANES = 128
V7X_VMEM_BYTES = 64 * 1024 * 1024
VMEM_LIMIT = V7X_VMEM_BYTES - 8 * 1024 * 1024

F32 = jnp.float32
BF16 = jnp.bfloat16


def _params(*sem):
    return pltpu.CompilerParams(dimension_semantics=sem, vmem_limit_bytes=VMEM_LIMIT)


def _sigmoid(x):
    return 1.0 / (1.0 + jnp.exp(-x))


def _silu(x):
    return x * _sigmoid(x)


def _rms(x, g):
    ms = jnp.mean(x * x, axis=-1, keepdims=True)
    return (x * lax.rsqrt(ms + EPS)) * g


def _modulated_norm(x, g, shift, scale):
    return _rms(x, g) * (1.0 + scale) + shift


def _dot(a, b):
    return jnp.dot(a, b, preferred_element_type=F32)


def _dot_nt(a, b):
    return lax.dot_general(a, b, (((1,), (1,)), ((), ())), preferred_element_type=F32)


def _dot_tn(a, b):
    return lax.dot_general(a, b, (((0,), (0,)), ((), ())), preferred_element_type=F32)


def _tile(n, pref):
    t = min(n, pref)
    assert n % t == 0, (n, t)
    return t


def _ada_kernel(c_ref, w_ref, b_ref, o_ref):
    ca = _silu(c_ref[...]).astype(BF16)
    o_ref[...] = _dot(ca, w_ref[...].astype(BF16)) + b_ref[...]


def _ada(c, w_ada, b_ada):
    L, D, N = w_ada.shape
    B = c.shape[0]
    tn = _tile(N, 1024)
    return pl.pallas_call(
        _ada_kernel,
        grid=(L, N // tn),
        in_specs=[
            pl.BlockSpec((B, D), lambda l, j: (0, 0)),
            pl.BlockSpec((None, D, tn), lambda l, j: (l, 0, j)),
            pl.BlockSpec((None, 1, tn), lambda l, j: (l, 0, j)),
        ],
        out_specs=pl.BlockSpec((None, B, tn), lambda l, j: (l, 0, j)),
        out_shape=jax.ShapeDtypeStruct((L, B, N), F32),
        compiler_params=_params("parallel", "parallel"),
    )(c, w_ada, b_ada.reshape(L, 1, N))


def _ffn_kernel(x_ref, mod_ref, g_ref, wg_ref, wu_ref, wd_ref, o_ref, h_ref, *, sub):
    j = pl.program_id(1)

    @pl.when(j == 0)
    def _():
        h = _modulated_norm(x_ref[...], g_ref[...], mod_ref[3 * sub:3 * sub + 1, :],
                            mod_ref[3 * sub + 1:3 * sub + 2, :])
        h_ref[...] = h.astype(BF16)
        o_ref[...] = jnp.zeros_like(o_ref)

    h = h_ref[...]
    gate = _dot(h, wg_ref[...])
    up = _dot(h, wu_ref[...])
    a = (_silu(gate) * up).astype(BF16)
    o_ref[...] += _dot(a, wd_ref[...])

    @pl.when(j == pl.num_programs(1) - 1)
    def _():
        o_ref[...] = x_ref[...] + (0.5 * mod_ref[3 * sub + 2:3 * sub + 3, :]) * o_ref[...]


def _ffn(x, mod_l, g, wg, wu, wd, layer, which, sub, seq):
    T, D = x.shape
    F = wg.shape[-1]
    tm = _tile(seq, 512)
    tf = _tile(F, 512)
    per_seq = seq // tm
    return pl.pallas_call(
        functools.partial(_ffn_kernel, sub=sub),
        grid=(T // tm, F // tf),
        in_specs=[
            pl.BlockSpec((tm, D), lambda i, j: (i, 0)),
            pl.BlockSpec((None, 3 * N_SUB, D), lambda i, j: (i // per_seq, 0, 0)),
            pl.BlockSpec((1, D), lambda i, j: (0, 0)),
            pl.BlockSpec((None, None, D, tf), lambda i, j: (layer, which, 0, j)),
            pl.BlockSpec((None, None, D, tf), lambda i, j: (layer, which, 0, j)),
            pl.BlockSpec((None, None, tf, D), lambda i, j: (layer, which, j, 0)),
        ],
        out_specs=pl.BlockSpec((tm, D), lambda i, j: (i, 0)),
        out_shape=jax.ShapeDtypeStruct((T, D), F32),
        scratch_shapes=[pltpu.VMEM((tm, D), BF16)],
        compiler_params=_params("parallel", "arbitrary"),
    )(x, mod_l, g, wg, wu, wd)


def _inproj_kernel(x_ref, mod_ref, g_ref, w_ref, o_ref, h_ref, *, sub, tn):
    j = pl.program_id(1)

    @pl.when(j == 0)
    def _():
        h = _modulated_norm(x_ref[...], g_ref[...], mod_ref[3 * sub:3 * sub + 1, :],
                            mod_ref[3 * sub + 1:3 * sub + 2, :])
        h_ref[...] = h.astype(BF16)

    o_ref[...] = _dot(h_ref[...], w_ref[:, pl.ds(pl.multiple_of(j * tn, V7X_LANES), tn)])


def _inproj(x, mod_l, g, w, seq):
    T, D = x.shape
    N = w.shape[-1]
    tm = _tile(seq, 512)
    tn = N // 3
    assert N % 3 == 0 and tn % V7X_LANES == 0
    per_seq = seq // tm
    return pl.pallas_call(
        functools.partial(_inproj_kernel, sub=1, tn=tn),
        grid=(T // tm, N // tn),
        in_specs=[
            pl.BlockSpec((tm, D), lambda i, j: (i, 0)),
            pl.BlockSpec((None, 3 * N_SUB, D), lambda i, j: (i // per_seq, 0, 0)),
            pl.BlockSpec((1, D), lambda i, j: (0, 0)),
            pl.BlockSpec((D, N), lambda i, j: (0, 0), pipeline_mode=pl.Buffered(1)),
        ],
        out_specs=pl.BlockSpec((tm, tn), lambda i, j: (i, j)),
        out_shape=jax.ShapeDtypeStruct((T, N), F32),
        scratch_shapes=[pltpu.VMEM((tm, D), BF16)],
        compiler_params=_params("parallel", "arbitrary"),
    )(x, mod_l, g, w)


def _hgrn_kernel(hq_ref, hf_ref, hi_ref, hg_ref, lb_ref, gn_ref, o_ref, st_ref, *, ts):
    C = HG_CHUNK
    nc = ts // C

    @pl.when(pl.program_id(2) == 0)
    def _():
        st_ref[...] = jnp.zeros_like(st_ref)

    z = hf_ref[...]
    lb = lb_ref[...]
    u = jnp.exp(-jnp.abs(z))
    r = 1.0 / (1.0 + u)
    ur = u * r
    z_pos = z >= 0.0
    log_f = jnp.log(lb + (1.0 - lb) * jnp.where(z_pos, r, ur))
    k = (1.0 - lb) * jnp.where(z_pos, ur, r)
    q = _silu(hq_ref[...])

    rows8 = 8
    per_chunk = C // rows8
    x = log_f.reshape(ts // rows8, rows8, HG_DIM)
    sub = lax.broadcasted_iota(jnp.int32, (1, rows8, HG_DIM), 1)
    step = 1
    while step < rows8:
        x = x + jnp.where(sub >= step, pltpu.roll(x, step, axis=1), 0.0)
        step *= 2
    x = x.reshape(nc, per_chunk, rows8, HG_DIM)
    parts = [x[:, 0]]
    carry = x[:, 0, rows8 - 1:rows8, :]
    for i in range(1, per_chunk):
        parts.append(x[:, i] + carry)
        if i + 1 < per_chunk:
            carry = carry + x[:, i, rows8 - 1:rows8, :]
    b3 = jnp.concatenate(parts, axis=1)

    b_end = b3[:, C - 1:C, :]
    q3 = q.reshape(nc, C, HG_DIM)
    k3 = k.reshape(nc, C, HG_DIM)
    q_in = (q3 * jnp.exp(b3)).astype(BF16).reshape(ts, HG_DIM)
    k_in = (k3 * jnp.exp(-b3)).astype(BF16).reshape(ts, HG_DIM)
    k_out = (k3 * jnp.exp(b_end - b3)).astype(BF16).reshape(ts, HG_DIM)
    decay = jnp.exp(b_end)
    v = hi_ref[...].astype(BF16)

    G = HG_GROUP
    R = G * C
    ri = lax.broadcasted_iota(jnp.int32, (R, R), 0)
    ci = lax.broadcasted_iota(jnp.int32, (R, R), 1)
    causal = (ci <= ri) & ((ri // C) == (ci // C))
    zeros = jnp.zeros((C, HG_DIM), BF16)

    def block_arranged(m):
        return jnp.concatenate(
            [jnp.concatenate([m[c * C:(c + 1) * C] if cc == c else zeros for cc in range(G)], axis=1)
             for c in range(G)], axis=0)

    st = st_ref[...]
    outs = []
    for g in range(ts // R):
        sl = slice(g * R, (g + 1) * R)
        a = jnp.where(causal, _dot_nt(q_in[sl], k_in[sl]), 0.0)
        o = _dot(a.astype(BF16), v[sl])
        kv = _dot_tn(v[sl], block_arranged(k_out[sl]))
        starts = []
        for c in range(G):
            starts.append(st)
            st = st * decay[g * G + c] + kv[:, c * HG_DIM:(c + 1) * HG_DIM]
        st_all = jnp.concatenate(starts, axis=1).astype(BF16)
        outs.append(o + _dot_nt(block_arranged(q_in[sl]), st_all))
    st_ref[...] = st

    o = jnp.concatenate(outs, axis=0)
    o_ref[...] = (_rms(o, gn_ref[...]) * _silu(hg_ref[...])).astype(BF16)


def _hgrn(proj, lb, gn, batch, seq):
    T = proj.shape[0]
    ts = _tile(seq, 2048)
    ns = seq // ts
    H = HG_HEADS

    def col(group):
        return pl.BlockSpec((ts, HG_DIM), lambda b, h, s: (b * ns + s, group * H + h))

    return pl.pallas_call(
        functools.partial(_hgrn_kernel, ts=ts),
        grid=(batch, H, ns),
        in_specs=[col(0), col(1), col(2), col(3),
                  pl.BlockSpec((1, HG_DIM), lambda b, h, s: (0, h)),
                  pl.BlockSpec((1, HG_DIM), lambda b, h, s: (0, 0))],
        out_specs=pl.BlockSpec((ts, HG_DIM), lambda b, h, s: (b * ns + s, h)),
        out_shape=jax.ShapeDtypeStruct((T, H * HG_DIM), BF16),
        scratch_shapes=[pltpu.VMEM((HG_DIM, HG_DIM), F32)],
        compiler_params=_params("parallel", "parallel", "arbitrary"),
    )(proj, proj, proj, proj, lb, gn)


def _mla_proj_kernel(qa_ref, kva_ref, kpe_ref, tab_ref, qg_ref, kg_ref, wq_ref, wk_ref, wv_ref,
                     qf_ref, kf_ref, v_ref, *, tm):
    scale = (MLA_NOPE + MLA_ROPE) ** -0.5
    tab = tab_ref[...]
    low = lax.broadcasted_iota(jnp.int32, (tm, V7X_LANES), 1) < MLA_ROPE

    def rope(y2):
        y = y2 * tab
        return jnp.where(low, y + pltpu.roll(y, MLA_ROPE, axis=1), 0.0)

    cq = _rms(qa_ref[...], qg_ref[...]).astype(BF16)
    qf = _dot(cq, wq_ref[...])
    ckv = _rms(kva_ref[...], kg_ref[...]).astype(BF16)
    kn = _dot(ckv, wk_ref[...])
    v_ref[...] = _dot(ckv, wv_ref[...]).astype(BF16)
    kr = rope(kpe_ref[...]).astype(BF16)
    P = MLA_QK_PAD
    for h in range(MLA_HEADS):
        qf_ref[:, h * P:h * P + MLA_NOPE] = (qf[:, h * P:h * P + MLA_NOPE] * scale).astype(BF16)
        qf_ref[:, h * P + MLA_NOPE:(h + 1) * P] = (rope(qf[:, h * P + MLA_NOPE:(h + 1) * P]) * scale).astype(BF16)
        kf_ref[:, h * P:h * P + MLA_NOPE] = kn[:, h * MLA_NOPE:(h + 1) * MLA_NOPE].astype(BF16)
        kf_ref[:, h * P + MLA_NOPE:(h + 1) * P] = kr


def _mla_proj(proj, tab, qg, kg, wq, wk, wv, seq):
    T = proj.shape[0]
    tm = _tile(seq, 512)
    qa_col = (4 * HG_HEADS * HG_DIM) // MLA_Q_RANK
    kva_col = (4 * HG_HEADS * HG_DIM + MLA_Q_RANK) // MLA_KV_RANK
    kpe_col = (4 * HG_HEADS * HG_DIM + MLA_Q_RANK + MLA_KV_RANK) // V7X_LANES
    full = lambda a: pl.BlockSpec(a.shape, lambda i: (0, 0))
    H = MLA_HEADS
    return pl.pallas_call(
        functools.partial(_mla_proj_kernel, tm=tm),
        grid=(T // tm,),
        in_specs=[
            pl.BlockSpec((tm, MLA_Q_RANK), lambda i: (i, qa_col)),
            pl.BlockSpec((tm, MLA_KV_RANK), lambda i: (i, kva_col)),
            pl.BlockSpec((tm, V7X_LANES), lambda i: (i, kpe_col)),
            pl.BlockSpec((tm, V7X_LANES), lambda i: (i, 0)),
            full(qg), full(kg), full(wq), full(wk), full(wv),
        ],
        out_specs=[
            pl.BlockSpec((tm, H * MLA_QK_PAD), lambda i: (i, 0)),
            pl.BlockSpec((tm, H * MLA_QK_PAD), lambda i: (i, 0)),
            pl.BlockSpec((tm, H * MLA_V), lambda i: (i, 0)),
        ],
        out_shape=[
            jax.ShapeDtypeStruct((T, H * MLA_QK_PAD), BF16),
            jax.ShapeDtypeStruct((T, H * MLA_QK_PAD), BF16),
            jax.ShapeDtypeStruct((T, H * MLA_V), BF16),
        ],
        compiler_params=_params("parallel"),
    )(proj, proj, proj, tab, qg, kg, wq, wk, wv)


def _attn_kernel(q_ref, k_ref, v_ref, o_ref, *, seq, t):
    n = seq // t
    keep = (lax.broadcasted_iota(jnp.int32, (t, t), 1)
            <= lax.broadcasted_iota(jnp.int32, (t, t), 0))
    for i in range(n):
        q = q_ref[i * t:(i + 1) * t, :]
        m = l = acc = None
        for j in range(i + 1):
            s = _dot_nt(q, k_ref[j * t:(j + 1) * t, :])
            if j == i:
                s = jnp.where(keep, s, -jnp.inf)
            v = v_ref[j * t:(j + 1) * t, :]
            row_max = jnp.max(s, axis=-1, keepdims=True)
            if j == 0:
                m = row_max
                p = jnp.exp(s - m)
                l = jnp.sum(p, axis=-1, keepdims=True)
                acc = _dot(p.astype(BF16), v)
            else:
                m_new = jnp.maximum(m, row_max)
                alpha = jnp.exp(m - m_new)
                p = jnp.exp(s - m_new)
                l = alpha * l + jnp.sum(p, axis=-1, keepdims=True)
                acc = alpha * acc + _dot(p.astype(BF16), v)
                m = m_new
        o_ref[i * t:(i + 1) * t, :] = (acc / l).astype(BF16)


def _attn(qf, kf, v, batch, seq):
    T = qf.shape[0]
    t = _tile(seq, 512)
    H = MLA_HEADS
    return pl.pallas_call(
        functools.partial(_attn_kernel, seq=seq, t=t),
        grid=(batch, H),
        in_specs=[
            pl.BlockSpec((seq, MLA_QK_PAD), lambda b, h: (b, h)),
            pl.BlockSpec((seq, MLA_QK_PAD), lambda b, h: (b, h)),
            pl.BlockSpec((seq, MLA_V), lambda b, h: (b, h)),
        ],
        out_specs=pl.BlockSpec((seq, MLA_V), lambda b, h: (b, h)),
        out_shape=jax.ShapeDtypeStruct((T, H * MLA_V), BF16),
        compiler_params=_params("parallel", "parallel"),
    )(qf, kf, v)


def _outproj_kernel(x_ref, mod_ref, a_ref, b_ref, wa_ref, wb_ref, o_ref, *, sub):
    y = _dot(a_ref[...], wa_ref[...]) + _dot(b_ref[...], wb_ref[...])
    o_ref[...] = x_ref[...] + mod_ref[3 * sub + 2:3 * sub + 3, :] * y


def _outproj(x, mod_l, a, b, wa, wb, seq):
    T, D = x.shape
    tm = _tile(seq, 512)
    per_seq = seq // tm
    return pl.pallas_call(
        functools.partial(_outproj_kernel, sub=1),
        grid=(T // tm,),
        in_specs=[
            pl.BlockSpec((tm, D), lambda i: (i, 0)),
            pl.BlockSpec((None, 3 * N_SUB, D), lambda i: (i // per_seq, 0, 0)),
            pl.BlockSpec((tm, a.shape[1]), lambda i: (i, 0)),
            pl.BlockSpec((tm, b.shape[1]), lambda i: (i, 0)),
            pl.BlockSpec(wa.shape, lambda i: (0, 0)),
            pl.BlockSpec(wb.shape, lambda i: (0, 0)),
        ],
        out_specs=pl.BlockSpec((tm, D), lambda i: (i, 0)),
        out_shape=jax.ShapeDtypeStruct((T, D), F32),
        compiler_params=_params("parallel"),
    )(x, mod_l, a, b, wa, wb)


def _final_kernel(x_ref, g_ref, o_ref):
    o_ref[...] = _rms(x_ref[...], g_ref[...])


def _final_norm(x, g, seq):
    T, D = x.shape
    tm = _tile(seq, 512)
    return pl.pallas_call(
        _final_kernel,
        grid=(T // tm,),
        in_specs=[pl.BlockSpec((tm, D), lambda i: (i, 0)), pl.BlockSpec((1, D), lambda i: (0, 0))],
        out_specs=pl.BlockSpec((tm, D), lambda i: (i, 0)),
        out_shape=jax.ShapeDtypeStruct((T, D), F32),
        compiler_params=_params("parallel"),
    )(x, g)


def _rotate_half_cols(w):
    half = MLA_ROPE // 2
    w1, w2 = w[..., :half], w[..., half:]
    return jnp.concatenate([-w2, w1], axis=-1)


def _prep_w_in(w_in):
    kpe = w_in[..., -MLA_ROPE:]
    return jnp.concatenate([w_in.astype(BF16), _rotate_half_cols(kpe).astype(BF16)], axis=-1)


def _prep_w_q(w_q_up):
    L, R, _ = w_q_up.shape
    w = w_q_up.reshape(L, R, MLA_HEADS, MLA_NOPE + MLA_ROPE)
    pe = w[..., MLA_NOPE:]
    w = jnp.concatenate([w, _rotate_half_cols(pe)], axis=-1)
    return w.reshape(L, R, MLA_HEADS * MLA_QK_PAD).astype(BF16)


def _prep_w_kv(w_kv_up):
    L, R, _ = w_kv_up.shape
    w = w_kv_up.reshape(L, R, MLA_HEADS, MLA_NOPE + MLA_V)
    wk = w[..., :MLA_NOPE].reshape(L, R, MLA_HEADS * MLA_NOPE)
    wv = w[..., MLA_NOPE:].reshape(L, R, MLA_HEADS * MLA_V)
    return wk.astype(BF16), wv.astype(BF16)


def kernel(x, c, positions, w_ada, b_ada, norm_g, w_in, qa_norm_g, w_q_up, kva_norm_g, w_kv_up,
           hg_lb_logits, hg_norm_g, w_out, ffn_w_gate, ffn_w_up, ffn_w_down, final_norm_g):
    B, S, D = x.shape
    L = w_ada.shape[0]
    T = B * S

    half = MLA_ROPE // 2
    inv_freq = ROPE_THETA ** (-jnp.arange(half, dtype=F32) / half)
    ang = positions.astype(F32).reshape(T, 1) * inv_freq
    cos, sin = jnp.cos(ang), jnp.sin(ang)
    tab = jnp.concatenate([cos, cos, sin, sin], axis=-1)

    lb_all = jnp.cumsum(jax.nn.softmax(hg_lb_logits.astype(F32), axis=0), axis=0)
    lb_all = lb_all - lb_all[0:1]

    w_in_p = _prep_w_in(w_in)
    w_q_p = _prep_w_q(w_q_up)
    w_k_p, w_v_p = _prep_w_kv(w_kv_up)
    hg_w = HG_HEADS * HG_DIM
    w_out_a = w_out[:, :hg_w, :].astype(BF16)
    w_out_b = w_out[:, hg_w:, :].astype(BF16)
    wg, wu, wd = ffn_w_gate.astype(BF16), ffn_w_up.astype(BF16), ffn_w_down.astype(BF16)

    mod = _ada(c, w_ada, b_ada).reshape(L, B, 3 * N_SUB, D)

    xf = x.reshape(T, D)
    for l in range(L):
        mod_l = mod[l]
        xf = _ffn(xf, mod_l, norm_g[l, 0:1], wg, wu, wd, l, 0, 0, S)
        proj = _inproj(xf, mod_l, norm_g[l, 1:2], w_in_p[l], S)
        o_hg = _hgrn(proj, lb_all[l:l + 1], hg_norm_g[l:l + 1], B, S)
        qf, kf, v = _mla_proj(proj, tab, qa_norm_g[l:l + 1], kva_norm_g[l:l + 1],
                              w_q_p[l], w_k_p[l], w_v_p[l], S)
        o_mla = _attn(qf, kf, v, B, S)
        xf = _outproj(xf, mod_l, o_hg, o_mla, w_out_a[l], w_out_b[l], S)
        xf = _ffn(xf, mod_l, norm_g[l, 2:3], wg, wu, wd, l, 1, 2, S)
    return _final_norm(xf, final_norm_g.reshape(1, D), S).reshape(B, S, D)
```

```python
import functools

import jax
import jax.numpy as jnp
from jax import lax
from jax.experimental import pallas as pl
from jax.experimental.pallas import tpu as pltpu

EPS = 1e-6
N_SUB = 3
HG_HEADS = 8
HG_DIM = 128
HG_CHUNK = 32
HG_GROUP = 4
MLA_HEADS = 8
MLA_NOPE = 128
MLA_ROPE = 64
MLA_V = 128
MLA_Q_RANK = 512
MLA_KV_RANK = 256
ROPE_THETA = 10000.0
MLA_QK_PAD = 256
FFN_DOWN_SLAB = 512
FFN_TM = 1024

V7X_LANES = 128
V7X_VMEM_BYTES = 64 * 1024 * 1024
VMEM_LIMIT = V7X_VMEM_BYTES - 3 * 1024 * 1024

F32 = jnp.float32
BF16 = jnp.bfloat16


def _params(*sem):
    return pltpu.CompilerParams(dimension_semantics=sem, vmem_limit_bytes=VMEM_LIMIT)


def _sigmoid(x):
    return 1.0 / (1.0 + jnp.exp(-x))


def _silu(x):
    return x * _sigmoid(x)


def _rms(x, g):
    ms = jnp.mean(x * x, axis=-1, keepdims=True)
    return (x * lax.rsqrt(ms + EPS)) * g


def _modulated_norm(x, g, shift, scale):
    return _rms(x, g) * (1.0 + scale) + shift


def _dot(a, b):
    return jnp.dot(a, b, preferred_element_type=F32)


def _dot_nt(a, b):
    return lax.dot_general(a, b, (((1,), (1,)), ((), ())), preferred_element_type=F32)


def _dot_tn(a, b):
    return lax.dot_general(a, b, (((0,), (0,)), ((), ())), preferred_element_type=F32)


def _tile(n, pref):
    t = min(n, pref)
    assert n % t == 0, (n, t)
    return t


def _ada_kernel(c_ref, w_ref, b_ref, o_ref):
    ca = _silu(c_ref[...]).astype(BF16)
    o_ref[...] = _dot(ca, w_ref[...].astype(BF16)) + b_ref[...]


def _ada(c, w_ada, b_ada):
    L, D, N = w_ada.shape
    B = c.shape[0]
    tn = _tile(N, 1024)
    return pl.pallas_call(
        _ada_kernel,
        grid=(L, N // tn),
        in_specs=[
            pl.BlockSpec((B, D), lambda l, j: (0, 0)),
            pl.BlockSpec((None, D, tn), lambda l, j: (l, 0, j)),
            pl.BlockSpec((None, 1, tn), lambda l, j: (l, 0, j)),
        ],
        out_specs=pl.BlockSpec((None, B, tn), lambda l, j: (l, 0, j)),
        out_shape=jax.ShapeDtypeStruct((L, B, N), F32),
        compiler_params=_params("parallel", "parallel"),
    )(c, w_ada, b_ada.reshape(L, 1, N))


def _ffn_kernel(x_hbm, mod_ref, modn_ref, g_ref, wg_ref, wu_ref, wd_ref, o_ref,
                xbuf, h_cur, h_next, sem, *, sub, tm, rows, n_slices):
    i = pl.program_id(0)
    j = pl.program_id(1)
    has_next = i + 1 < pl.num_programs(0)
    shift, scale, gate_row = 3 * sub, 3 * sub + 1, 3 * sub + 2

    def fetch(tile):
        return pltpu.make_async_copy(x_hbm.at[pl.ds(tile * tm, tm), :], xbuf.at[pl.ds(0, tm), :], sem)

    def norm_rows(r0, nrows, mod_src):
        h = _modulated_norm(xbuf[pl.ds(r0, nrows), :], g_ref[...],
                            mod_src[shift:shift + 1, :], mod_src[scale:scale + 1, :])
        h_next[pl.ds(r0, nrows), :] = h.astype(BF16)
        return h

    @pl.when((i == 0) & (j == 0))
    def _():
        first = fetch(0)
        first.start()
        xbuf[pl.ds(tm, rows), :] = jnp.zeros((rows, xbuf.shape[1]), F32)
        first.wait()
        for s in range(n_slices):
            norm_rows(s * rows, rows, mod_ref)

    @pl.when(j == 0)
    def _():
        o_ref[...] = xbuf[pl.ds(0, tm), :]
        h_cur[...] = h_next[pl.ds(0, tm), :]

    @pl.when((j == 2) & has_next)
    def _():
        fetch(i + 1).wait()

    scheduled = (j >= 2) & (j < 2 + n_slices)
    r0 = pl.multiple_of(jnp.where(scheduled, (j - 2) * rows, tm), rows)
    hn = norm_rows(r0, rows, modn_ref)
    d_model = o_ref.shape[1]
    tf = wg_ref.shape[1]
    bits = pltpu.bitcast(hn, jnp.uint32).reshape(rows // 8, 8, d_model)
    folded = bits[0]
    for r in range(1, rows // 8):
        folded = folded | bits[r]
    lanes = folded[:, 0:tf]
    for k in range(1, d_model // tf):
        lanes = lanes | folded[:, k * tf:(k + 1) * tf]
    half_word = jnp.uint32(16)
    zero = pltpu.bitcast(lax.shift_right_logical(lax.shift_right_logical(lanes, half_word), half_word), F32)

    h = h_cur[...]
    gate = _dot(h, wg_ref[...])
    up = (_dot(h, wu_ref[...]).reshape(tm // 8, 8, tf) + zero).reshape(tm, tf)
    a = (_silu(gate) * up).astype(BF16)
    c = 0.5 * mod_ref[gate_row:gate_row + 1, :]
    slab = min(d_model, FFN_DOWN_SLAB)
    for n in range(d_model // slab):
        sl = slice(n * slab, (n + 1) * slab)
        o_ref[:, sl] += c[:, sl] * _dot(a, wd_ref[:, sl])

    @pl.when((j == 0) & has_next)
    def _():
        fetch(i + 1).start()


def _ffn(x, mod_l, g, wg, wu, wd, layer, which, sub, seq):
    T, D = x.shape
    F = wg.shape[-1]
    tm = _tile(seq, FFN_TM)
    tf = _tile(F, 512)
    nt, nf = T // tm, F // tf
    per_seq = seq // tm
    n_slices = 1
    while 2 * n_slices <= min(nf - 2, 8):
        n_slices *= 2
    assert nf >= 2 + n_slices and tm % (16 * n_slices) == 0, (nf, tm)
    rows = tm // n_slices
    mod_spec = lambda tile_of: pl.BlockSpec((None, 3 * N_SUB, D), lambda i, j: (tile_of(i) // per_seq, 0, 0))
    return pl.pallas_call(
        functools.partial(_ffn_kernel, sub=sub, tm=tm, rows=rows, n_slices=n_slices),
        grid=(nt, nf),
        in_specs=[
            pl.BlockSpec(memory_space=pl.ANY),
            mod_spec(lambda i: i),
            mod_spec(lambda i: jnp.minimum(i + 1, nt - 1)),
            pl.BlockSpec((1, D), lambda i, j: (0, 0)),
            pl.BlockSpec((None, None, D, tf), lambda i, j: (layer, which, 0, j)),
            pl.BlockSpec((None, None, D, tf), lambda i, j: (layer, which, 0, j)),
            pl.BlockSpec((None, None, tf, D), lambda i, j: (layer, which, j, 0)),
        ],
        out_specs=pl.BlockSpec((tm, D), lambda i, j: (i, 0)),
        out_shape=jax.ShapeDtypeStruct((T, D), F32),
        scratch_shapes=[pltpu.VMEM((tm + rows, D), F32), pltpu.VMEM((tm, D), BF16),
                        pltpu.VMEM((tm + rows, D), BF16), pltpu.SemaphoreType.DMA(())],
        compiler_params=_params("arbitrary", "arbitrary"),
    )(x, mod_l, mod_l, g, wg, wu, wd)


def _inproj_kernel(x_ref, mod_ref, g_ref, w_ref, o_ref, h_ref, *, sub, tn):
    j = pl.program_id(1)

    @pl.when(j == 0)
    def _():
        h = _modulated_norm(x_ref[...], g_ref[...], mod_ref[3 * sub:3 * sub + 1, :],
                            mod_ref[3 * sub + 1:3 * sub + 2, :])
        h_ref[...] = h.astype(BF16)

    o_ref[...] = _dot(h_ref[...], w_ref[:, pl.ds(pl.multiple_of(j * tn, V7X_LANES), tn)])


def _inproj(x, mod_l, g, w, seq):
    T, D = x.shape
    N = w.shape[-1]
    tm = _tile(seq, 512)
    tn = N // 3
    assert N % 3 == 0 and tn % V7X_LANES == 0
    per_seq = seq // tm
    return pl.pallas_call(
        functools.partial(_inproj_kernel, sub=1, tn=tn),
        grid=(T // tm, N // tn),
        in_specs=[
            pl.BlockSpec((tm, D), lambda i, j: (i, 0)),
            pl.BlockSpec((None, 3 * N_SUB, D), lambda i, j: (i // per_seq, 0, 0)),
            pl.BlockSpec((1, D), lambda i, j: (0, 0)),
            pl.BlockSpec((D, N), lambda i, j: (0, 0), pipeline_mode=pl.Buffered(1)),
        ],
        out_specs=pl.BlockSpec((tm, tn), lambda i, j: (i, j)),
        out_shape=jax.ShapeDtypeStruct((T, N), F32),
        scratch_shapes=[pltpu.VMEM((tm, D), BF16)],
        compiler_params=_params("parallel", "arbitrary"),
    )(x, mod_l, g, w)


def _hgrn_kernel(hq_ref, hf_ref, hi_ref, hg_ref, lb_ref, gn_ref, o_ref, st_ref, *, ts):
    C = HG_CHUNK
    nc = ts // C

    @pl.when(pl.program_id(2) == 0)
    def _():
        st_ref[...] = jnp.zeros_like(st_ref)

    z = hf_ref[...]
    lb = lb_ref[...]
    u = jnp.exp(-jnp.abs(z))
    r = 1.0 / (1.0 + u)
    ur = u * r
    z_pos = z >= 0.0
    log_f = jnp.log(lb + (1.0 - lb) * jnp.where(z_pos, r, ur))
    k = (1.0 - lb) * jnp.where(z_pos, ur, r)
    q = _silu(hq_ref[...])

    rows8 = 8
    per_chunk = C // rows8
    x = log_f.reshape(ts // rows8, rows8, HG_DIM)
    sub = lax.broadcasted_iota(jnp.int32, (1, rows8, HG_DIM), 1)
    step = 1
    while step < rows8:
        x = x + jnp.where(sub >= step, pltpu.roll(x, step, axis=1), 0.0)
        step *= 2
    x = x.reshape(nc, per_chunk, rows8, HG_DIM)
    parts = [x[:, 0]]
    carry = x[:, 0, rows8 - 1:rows8, :]
    for i in range(1, per_chunk):
        parts.append(x[:, i] + carry)
        if i + 1 < per_chunk:
            carry = carry + x[:, i, rows8 - 1:rows8, :]
    b3 = jnp.concatenate(parts, axis=1)

    b_end = b3[:, C - 1:C, :]
    q3 = q.reshape(nc, C, HG_DIM)
    k3 = k.reshape(nc, C, HG_DIM)
    q_in = (q3 * jnp.exp(b3)).astype(BF16).reshape(ts, HG_DIM)
    k_in = (k3 * jnp.exp(-b3)).astype(BF16).reshape(ts, HG_DIM)
    k_out = (k3 * jnp.exp(b_end - b3)).astype(BF16).reshape(ts, HG_DIM)
    decay = jnp.exp(b_end)
    v = hi_ref[...].astype(BF16)

    G = HG_GROUP
    R = G * C
    ri = lax.broadcasted_iota(jnp.int32, (R, R), 0)
    ci = lax.broadcasted_iota(jnp.int32, (R, R), 1)
    causal = (ci <= ri) & ((ri // C) == (ci // C))
    zeros = jnp.zeros((C, HG_DIM), BF16)

    def block_arranged(m):
        return jnp.concatenate(
            [jnp.concatenate([m[c * C:(c + 1) * C] if cc == c else zeros for cc in range(G)], axis=1)
             for c in range(G)], axis=0)

    st = st_ref[...]
    outs = []
    for g in range(ts // R):
        sl = slice(g * R, (g + 1) * R)
        a = jnp.where(causal, _dot_nt(q_in[sl], k_in[sl]), 0.0)
        o = _dot(a.astype(BF16), v[sl])
        kv = _dot_tn(v[sl], block_arranged(k_out[sl]))
        starts = []
        for c in range(G):
            starts.append(st)
            st = st * decay[g * G + c] + kv[:, c * HG_DIM:(c + 1) * HG_DIM]
        st_all = jnp.concatenate(starts, axis=1).astype(BF16)
        outs.append(o + _dot_nt(block_arranged(q_in[sl]), st_all))
    st_ref[...] = st

    o = jnp.concatenate(outs, axis=0)
    o_ref[...] = (_rms(o, gn_ref[...]) * _silu(hg_ref[...])).astype(BF16)


def _hgrn(proj, lb, gn, batch, seq):
    T = proj.shape[0]
    ts = _tile(seq, 2048)
    ns = seq // ts
    H = HG_HEADS

    def col(group):
        return pl.BlockSpec((ts, HG_DIM), lambda b, h, s: (b * ns + s, group * H + h))

    return pl.pallas_call(
        functools.partial(_hgrn_kernel, ts=ts),
        grid=(batch, H, ns),
        in_specs=[col(0), col(1), col(2), col(3),
                  pl.BlockSpec((1, HG_DIM), lambda b, h, s: (0, h)),
                  pl.BlockSpec((1, HG_DIM), lambda b, h, s: (0, 0))],
        out_specs=pl.BlockSpec((ts, HG_DIM), lambda b, h, s: (b * ns + s, h)),
        out_shape=jax.ShapeDtypeStruct((T, H * HG_DIM), BF16),
        scratch_shapes=[pltpu.VMEM((HG_DIM, HG_DIM), F32)],
        compiler_params=_params("parallel", "parallel", "arbitrary"),
    )(proj, proj, proj, proj, lb, gn)


def _mla_proj_kernel(qa_ref, kva_ref, kpe_ref, tab_ref, qg_ref, kg_ref, wq_ref, wk_ref, wv_ref,
                     qf_ref, kf_ref, v_ref, *, tm):
    scale = (MLA_NOPE + MLA_ROPE) ** -0.5
    tab = tab_ref[...]
    low = lax.broadcasted_iota(jnp.int32, (tm, V7X_LANES), 1) < MLA_ROPE

    def rope(y2):
        y = y2 * tab
        return jnp.where(low, y + pltpu.roll(y, MLA_ROPE, axis=1), 0.0)

    cq = _rms(qa_ref[...], qg_ref[...]).astype(BF16)
    qf = _dot(cq, wq_ref[...])
    ckv = _rms(kva_ref[...], kg_ref[...]).astype(BF16)
    kn = _dot(ckv, wk_ref[...])
    v_ref[...] = _dot(ckv, wv_ref[...]).astype(BF16)
    kr = rope(kpe_ref[...]).astype(BF16)
    P = MLA_QK_PAD
    for h in range(MLA_HEADS):
        qf_ref[:, h * P:h * P + MLA_NOPE] = (qf[:, h * P:h * P + MLA_NOPE] * scale).astype(BF16)
        qf_ref[:, h * P + MLA_NOPE:(h + 1) * P] = (rope(qf[:, h * P + MLA_NOPE:(h + 1) * P]) * scale).astype(BF16)
        kf_ref[:, h * P:h * P + MLA_NOPE] = kn[:, h * MLA_NOPE:(h + 1) * MLA_NOPE].astype(BF16)
        kf_ref[:, h * P + MLA_NOPE:(h + 1) * P] = kr


def _mla_proj(proj, tab, qg, kg, wq, wk, wv, seq):
    T = proj.shape[0]
    tm = _tile(seq, 512)
    qa_col = (4 * HG_HEADS * HG_DIM) // MLA_Q_RANK
    kva_col = (4 * HG_HEADS * HG_DIM + MLA_Q_RANK) // MLA_KV_RANK
    kpe_col = (4 * HG_HEADS * HG_DIM + MLA_Q_RANK + MLA_KV_RANK) // V7X_LANES
    full = lambda a: pl.BlockSpec(a.shape, lambda i: (0, 0))
    H = MLA_HEADS
    return pl.pallas_call(
        functools.partial(_mla_proj_kernel, tm=tm),
        grid=(T // tm,),
        in_specs=[
            pl.BlockSpec((tm, MLA_Q_RANK), lambda i: (i, qa_col)),
            pl.BlockSpec((tm, MLA_KV_RANK), lambda i: (i, kva_col)),
            pl.BlockSpec((tm, V7X_LANES), lambda i: (i, kpe_col)),
            pl.BlockSpec((tm, V7X_LANES), lambda i: (i, 0)),
            full(qg), full(kg), full(wq), full(wk), full(wv),
        ],
        out_specs=[
            pl.BlockSpec((tm, H * MLA_QK_PAD), lambda i: (i, 0)),
            pl.BlockSpec((tm, H * MLA_QK_PAD), lambda i: (i, 0)),
            pl.BlockSpec((tm, H * MLA_V), lambda i: (i, 0)),
        ],
        out_shape=[
            jax.ShapeDtypeStruct((T, H * MLA_QK_PAD), BF16),
            jax.ShapeDtypeStruct((T, H * MLA_QK_PAD), BF16),
            jax.ShapeDtypeStruct((T, H * MLA_V), BF16),
        ],
        compiler_params=_params("parallel"),
    )(proj, proj, proj, tab, qg, kg, wq, wk, wv)


def _attn_kernel(q_ref, k_ref, v_ref, o_ref, *, seq, t):
    n = seq // t
    keep = (lax.broadcasted_iota(jnp.int32, (t, t), 1)
            <= lax.broadcasted_iota(jnp.int32, (t, t), 0))
    for i in range(n):
        q = q_ref[i * t:(i + 1) * t, :]
        m = l = acc = None
        for j in range(i + 1):
            s = _dot_nt(q, k_ref[j * t:(j + 1) * t, :])
            if j == i:
                s = jnp.where(keep, s, -jnp.inf)
            v = v_ref[j * t:(j + 1) * t, :]
            row_max = jnp.max(s, axis=-1, keepdims=True)
            if j == 0:
                m = row_max
                p = jnp.exp(s - m)
                l = jnp.sum(p, axis=-1, keepdims=True)
                acc = _dot(p.astype(BF16), v)
            else:
                m_new = jnp.maximum(m, row_max)
                alpha = jnp.exp(m - m_new)
                p = jnp.exp(s - m_new)
                l = alpha * l + jnp.sum(p, axis=-1, keepdims=True)
                acc = alpha * acc + _dot(p.astype(BF16), v)
                m = m_new
        o_ref[i * t:(i + 1) * t, :] = (acc / l).astype(BF16)


def _attn(qf, kf, v, batch, seq):
    T = qf.shape[0]
    t = _tile(seq, 512)
    H = MLA_HEADS
    return pl.pallas_call(
        functools.partial(_attn_kernel, seq=seq, t=t),
        grid=(batch, H),
        in_specs=[
            pl.BlockSpec((seq, MLA_QK_PAD), lambda b, h: (b, h)),
            pl.BlockSpec((seq, MLA_QK_PAD), lambda b, h: (b, h)),
            pl.BlockSpec((seq, MLA_V), lambda b, h: (b, h)),
        ],
        out_specs=pl.BlockSpec((seq, MLA_V), lambda b, h: (b, h)),
        out_shape=jax.ShapeDtypeStruct((T, H * MLA_V), BF16),
        compiler_params=_params("parallel", "parallel"),
    )(qf, kf, v)


def _outproj_kernel(x_ref, mod_ref, a_ref, b_ref, wa_ref, wb_ref, o_ref, *, sub):
    y = _dot(a_ref[...], wa_ref[...]) + _dot(b_ref[...], wb_ref[...])
    o_ref[...] = x_ref[...] + mod_ref[3 * sub + 2:3 * sub + 3, :] * y


def _outproj(x, mod_l, a, b, wa, wb, seq):
    T, D = x.shape
    tm = _tile(seq, 512)
    per_seq = seq // tm
    return pl.pallas_call(
        functools.partial(_outproj_kernel, sub=1),
        grid=(T // tm,),
        in_specs=[
            pl.BlockSpec((tm, D), lambda i: (i, 0)),
            pl.BlockSpec((None, 3 * N_SUB, D), lambda i: (i // per_seq, 0, 0)),
            pl.BlockSpec((tm, a.shape[1]), lambda i: (i, 0)),
            pl.BlockSpec((tm, b.shape[1]), lambda i: (i, 0)),
            pl.BlockSpec(wa.shape, lambda i: (0, 0)),
            pl.BlockSpec(wb.shape, lambda i: (0, 0)),
        ],
        out_specs=pl.BlockSpec((tm, D), lambda i: (i, 0)),
        out_shape=jax.ShapeDtypeStruct((T, D), F32),
        compiler_params=_params("parallel"),
    )(x, mod_l, a, b, wa, wb)


def _final_kernel(x_ref, g_ref, o_ref):
    o_ref[...] = _rms(x_ref[...], g_ref[...])


def _final_norm(x, g, seq):
    T, D = x.shape
    tm = _tile(seq, 512)
    return pl.pallas_call(
        _final_kernel,
        grid=(T // tm,),
        in_specs=[pl.BlockSpec((tm, D), lambda i: (i, 0)), pl.BlockSpec((1, D), lambda i: (0, 0))],
        out_specs=pl.BlockSpec((tm, D), lambda i: (i, 0)),
        out_shape=jax.ShapeDtypeStruct((T, D), F32),
        compiler_params=_params("parallel"),
    )(x, g)


def _rotate_half_cols(w):
    half = MLA_ROPE // 2
    w1, w2 = w[..., :half], w[..., half:]
    return jnp.concatenate([-w2, w1], axis=-1)


def _prep_w_in(w_in):
    kpe = w_in[..., -MLA_ROPE:]
    return jnp.concatenate([w_in.astype(BF16), _rotate_half_cols(kpe).astype(BF16)], axis=-1)


def _prep_w_q(w_q_up):
    L, R, _ = w_q_up.shape
    w = w_q_up.reshape(L, R, MLA_HEADS, MLA_NOPE + MLA_ROPE)
    pe = w[..., MLA_NOPE:]
    w = jnp.concatenate([w, _rotate_half_cols(pe)], axis=-1)
    return w.reshape(L, R, MLA_HEADS * MLA_QK_PAD).astype(BF16)


def _prep_w_kv(w_kv_up):
    L, R, _ = w_kv_up.shape
    w = w_kv_up.reshape(L, R, MLA_HEADS, MLA_NOPE + MLA_V)
    wk = w[..., :MLA_NOPE].reshape(L, R, MLA_HEADS * MLA_NOPE)
    wv = w[..., MLA_NOPE:].reshape(L, R, MLA_HEADS * MLA_V)
    return wk.astype(BF16), wv.astype(BF16)


def kernel(x, c, positions, w_ada, b_ada, norm_g, w_in, qa_norm_g, w_q_up, kva_norm_g, w_kv_up,
           hg_lb_logits, hg_norm_g, w_out, ffn_w_gate, ffn_w_up, ffn_w_down, final_norm_g):
    B, S, D = x.shape
    L = w_ada.shape[0]
    T = B * S

    half = MLA_ROPE // 2
    inv_freq = ROPE_THETA ** (-jnp.arange(half, dtype=F32) / half)
    ang = positions.astype(F32).reshape(T, 1) * inv_freq
    cos, sin = jnp.cos(ang), jnp.sin(ang)
    tab = jnp.concatenate([cos, cos, sin, sin], axis=-1)

    lb_all = jnp.cumsum(jax.nn.softmax(hg_lb_logits.astype(F32), axis=0), axis=0)
    lb_all = lb_all - lb_all[0:1]

    w_in_p = _prep_w_in(w_in)
    w_q_p = _prep_w_q(w_q_up)
    w_k_p, w_v_p = _prep_w_kv(w_kv_up)
    hg_w = HG_HEADS * HG_DIM
    w_out_a = w_out[:, :hg_w, :].astype(BF16)
    w_out_b = w_out[:, hg_w:, :].astype(BF16)
    wg, wu, wd = ffn_w_gate.astype(BF16), ffn_w_up.astype(BF16), ffn_w_down.astype(BF16)

    mod = _ada(c, w_ada, b_ada).reshape(L, B, 3 * N_SUB, D)

    xf = x.reshape(T, D)
    for l in range(L):
        mod_l = mod[l]
        xf = _ffn(xf, mod_l, norm_g[l, 0:1], wg, wu, wd, l, 0, 0, S)
        proj = _inproj(xf, mod_l, norm_g[l, 1:2], w_in_p[l], S)
        o_hg = _hgrn(proj, lb_all[l:l + 1], hg_norm_g[l:l + 1], B, S)
        qf, kf, v = _mla_proj(proj, tab, qa_norm_g[l:l + 1], kva_norm_g[l:l + 1],
                              w_q_p[l], w_k_p[l], w_v_p[l], S)
        o_mla = _attn(qf, kf, v, B, S)
        xf = _outproj(xf, mod_l, o_hg, o_mla, w_out_a[l], w_out_b[l], S)
        xf = _ffn(xf, mod_l, norm_g[l, 2:3], wg, wu, wd, l, 1, 2, S)
    return _final_norm(xf, final_norm_g.reshape(1, D), S).reshape(B, S, D)
```

```python
import functools

import jax
import jax.numpy as jnp
from jax import lax
from jax.experimental import pallas as pl
from jax.experimental.pallas import tpu as pltpu

EPS = 1e-6
N_SUB = 3
HG_HEADS = 8
HG_DIM = 128
HG_CHUNK = 32
HG_GROUP = 4
MLA_HEADS = 8
MLA_NOPE = 128
MLA_ROPE = 64
MLA_V = 128
MLA_Q_RANK = 512
MLA_KV_RANK = 256
ROPE_THETA = 10000.0
MLA_QK_PAD = 256
FFN_DOWN_SLAB = 512
FFN_TM = 1024

V7X_LANES = 128
V7X_MXU_DIM = 256
V7X_VMEM_BYTES = 64 * 1024 * 1024
VMEM_LIMIT = V7X_VMEM_BYTES - 3 * 1024 * 1024

F32 = jnp.float32
BF16 = jnp.bfloat16


def _params(*sem):
    return pltpu.CompilerParams(dimension_semantics=sem, vmem_limit_bytes=VMEM_LIMIT)


def _sigmoid(x):
    return 1.0 / (1.0 + jnp.exp(-x))


def _silu(x):
    return x * _sigmoid(x)


def _rms(x, g):
    ms = jnp.mean(x * x, axis=-1, keepdims=True)
    return (x * lax.rsqrt(ms + EPS)) * g


def _modulated_norm(x, g, shift, scale):
    return _rms(x, g) * (1.0 + scale) + shift


def _dot(a, b):
    return jnp.dot(a, b, preferred_element_type=F32)


def _dot_nt(a, b):
    return lax.dot_general(a, b, (((1,), (1,)), ((), ())), preferred_element_type=F32)


def _dot_tn(a, b):
    return lax.dot_general(a, b, (((0,), (0,)), ((), ())), preferred_element_type=F32)


def _tile(n, pref):
    t = min(n, pref)
    assert n % t == 0, (n, t)
    return t


def _ada_kernel(c_ref, w_ref, b_ref, o_ref):
    ca = _silu(c_ref[...]).astype(BF16)
    o_ref[...] = _dot(ca, w_ref[...].astype(BF16)) + b_ref[...]


def _ada(c, w_ada, b_ada):
    L, D, N = w_ada.shape
    B = c.shape[0]
    tn = _tile(N, 1024)
    return pl.pallas_call(
        _ada_kernel,
        grid=(L, N // tn),
        in_specs=[
            pl.BlockSpec((B, D), lambda l, j: (0, 0)),
            pl.BlockSpec((None, D, tn), lambda l, j: (l, 0, j)),
            pl.BlockSpec((None, 1, tn), lambda l, j: (l, 0, j)),
        ],
        out_specs=pl.BlockSpec((None, B, tn), lambda l, j: (l, 0, j)),
        out_shape=jax.ShapeDtypeStruct((L, B, N), F32),
        compiler_params=_params("parallel", "parallel"),
    )(c, w_ada, b_ada.reshape(L, 1, N))


def _ffn_kernel(x_hbm, mod_ref, modn_ref, g_ref, wg_ref, wu_ref, wd_ref, fin_ref, o_ref,
                xbuf, h_cur, h_next, sem, *, sub, tm, rows, n_slices, final):
    i = pl.program_id(0)
    j = pl.program_id(1)
    has_next = i + 1 < pl.num_programs(0)
    shift, scale, gate_row = 3 * sub, 3 * sub + 1, 3 * sub + 2

    def fetch(tile):
        return pltpu.make_async_copy(x_hbm.at[pl.ds(tile * tm, tm), :], xbuf.at[pl.ds(0, tm), :], sem)

    def norm_rows(r0, nrows, mod_src):
        h = _modulated_norm(xbuf[pl.ds(r0, nrows), :], g_ref[...],
                            mod_src[shift:shift + 1, :], mod_src[scale:scale + 1, :])
        h_next[pl.ds(r0, nrows), :] = h.astype(BF16)
        return h

    @pl.when((i == 0) & (j == 0))
    def _():
        first = fetch(0)
        first.start()
        xbuf[pl.ds(tm, rows), :] = jnp.zeros((rows, xbuf.shape[1]), F32)
        first.wait()
        for s in range(n_slices):
            norm_rows(s * rows, rows, mod_ref)

    @pl.when(j == 0)
    def _():
        o_ref[...] = xbuf[pl.ds(0, tm), :]
        h_cur[...] = h_next[pl.ds(0, tm), :]

    @pl.when((j == 2) & has_next)
    def _():
        fetch(i + 1).wait()

    scheduled = (j >= 2) & (j < 2 + n_slices)
    r0 = pl.multiple_of(jnp.where(scheduled, (j - 2) * rows, tm), rows)
    hn = norm_rows(r0, rows, modn_ref)
    d_model = o_ref.shape[1]
    tf = wg_ref.shape[1]
    bits = pltpu.bitcast(hn, jnp.uint32).reshape(rows // 8, 8, d_model)
    folded = bits[0]
    for r in range(1, rows // 8):
        folded = folded | bits[r]
    lanes = folded[:, 0:tf]
    for k in range(1, d_model // tf):
        lanes = lanes | folded[:, k * tf:(k + 1) * tf]
    half_word = jnp.uint32(16)
    zero = pltpu.bitcast(lax.shift_right_logical(lax.shift_right_logical(lanes, half_word), half_word), F32)

    h = h_cur[...]
    gate = _dot(h, wg_ref[...])
    up = (_dot(h, wu_ref[...]).reshape(tm // 8, 8, tf) + zero).reshape(tm, tf)
    a = (_silu(gate) * up).astype(BF16)
    c = 0.5 * mod_ref[gate_row:gate_row + 1, :]
    slab = min(d_model, FFN_DOWN_SLAB)
    for n in range(d_model // slab):
        sl = slice(n * slab, (n + 1) * slab)
        o_ref[:, sl] += c[:, sl] * _dot(a, wd_ref[:, sl])

    @pl.when((j == 0) & has_next)
    def _():
        fetch(i + 1).start()

    if final:
        @pl.when(j == pl.num_programs(1) - 1)
        def _():
            o_ref[...] = _rms(o_ref[...], fin_ref[...])


def _ffn(x, mod_l, g, wg, wu, wd, fin_g, layer, which, sub, seq, final):
    T, D = x.shape
    F = wg.shape[-1]
    tm = _tile(seq, FFN_TM)
    tf = _tile(F, 512)
    nt, nf = T // tm, F // tf
    per_seq = seq // tm
    n_slices = 1
    while 2 * n_slices <= min(nf - 2, 8):
        n_slices *= 2
    assert nf >= 2 + n_slices and tm % (16 * n_slices) == 0, (nf, tm)
    rows = tm // n_slices
    mod_spec = lambda tile_of: pl.BlockSpec((None, 3 * N_SUB, D), lambda i, j: (tile_of(i) // per_seq, 0, 0))
    return pl.pallas_call(
        functools.partial(_ffn_kernel, sub=sub, tm=tm, rows=rows, n_slices=n_slices, final=final),
        grid=(nt, nf),
        in_specs=[
            pl.BlockSpec(memory_space=pl.ANY),
            mod_spec(lambda i: i),
            mod_spec(lambda i: jnp.minimum(i + 1, nt - 1)),
            pl.BlockSpec((1, D), lambda i, j: (0, 0)),
            pl.BlockSpec((None, None, D, tf), lambda i, j: (layer, which, 0, j)),
            pl.BlockSpec((None, None, D, tf), lambda i, j: (layer, which, 0, j)),
            pl.BlockSpec((None, None, tf, D), lambda i, j: (layer, which, j, 0)),
            pl.BlockSpec((1, D), lambda i, j: (0, 0)),
        ],
        out_specs=pl.BlockSpec((tm, D), lambda i, j: (i, 0)),
        out_shape=jax.ShapeDtypeStruct((T, D), F32),
        scratch_shapes=[pltpu.VMEM((tm + rows, D), F32), pltpu.VMEM((tm, D), BF16),
                        pltpu.VMEM((tm + rows, D), BF16), pltpu.SemaphoreType.DMA(())],
        compiler_params=_params("arbitrary", "arbitrary"),
    )(x, mod_l, mod_l, g, wg, wu, wd, fin_g)


def _inproj_kernel(x_hbm, mod_ref, modn_ref, g_ref, w_ref, o_ref, xbuf, h_cur, h_next, sem,
                   *, sub, tm, tn, rows, n_slices):
    i = pl.program_id(0)
    j = pl.program_id(1)
    has_next = i + 1 < pl.num_programs(0)
    shift, scale = 3 * sub, 3 * sub + 1

    def fetch(tile):
        return pltpu.make_async_copy(x_hbm.at[pl.ds(tile * tm, tm), :], xbuf.at[pl.ds(0, tm), :], sem)

    def norm_rows(r0, nrows, mod_src):
        h = _modulated_norm(xbuf[pl.ds(r0, nrows), :], g_ref[...],
                            mod_src[shift:shift + 1, :], mod_src[scale:scale + 1, :])
        h_next[pl.ds(r0, nrows), :] = h.astype(BF16)
        return h

    @pl.when((i == 0) & (j == 0))
    def _():
        first = fetch(0)
        first.start()
        xbuf[pl.ds(tm, rows), :] = jnp.zeros((rows, xbuf.shape[1]), F32)
        first.wait()
        for s in range(n_slices):
            norm_rows(s * rows, rows, mod_ref)

    @pl.when(j == 0)
    def _():
        h_cur[...] = h_next[pl.ds(0, tm), :]

    @pl.when((j == 0) & has_next)
    def _():
        fetch(i + 1).start()

    @pl.when((j == 1) & has_next)
    def _():
        fetch(i + 1).wait()

    scheduled = (j >= 1) & (j < 1 + n_slices)
    r0 = pl.multiple_of(jnp.where(scheduled, (j - 1) * rows, tm), rows)
    hn = norm_rows(r0, rows, modn_ref)
    d_model = xbuf.shape[1]
    bits = pltpu.bitcast(hn, jnp.uint32).reshape(rows // 8, 8, d_model)
    folded = bits[0]
    for r in range(1, rows // 8):
        folded = folded | bits[r]
    lanes = folded[:, 0:V7X_LANES]
    for k in range(1, d_model // V7X_LANES):
        lanes = lanes | folded[:, k * V7X_LANES:(k + 1) * V7X_LANES]
    half_word = jnp.uint32(16)
    zero = pltpu.bitcast(lax.shift_right_logical(lax.shift_right_logical(lanes, half_word), half_word), F32)
    zero = jnp.concatenate([zero] * (tn // V7X_LANES), axis=1)

    y = _dot(h_cur[...], w_ref[:, pl.ds(pl.multiple_of(j * tn, V7X_LANES), tn)])
    o_ref[...] = (y.reshape(tm // 8, 8, tn) + zero).reshape(tm, tn)


def _inproj(x, mod_l, g, w, layer, seq):
    T, D = x.shape
    N = w.shape[-1]
    tm = _tile(seq, 512)
    nj = 2
    tn = N // nj
    assert N % nj == 0 and tn % V7X_LANES == 0
    nt = T // tm
    per_seq = seq // tm
    n_slices = nj - 1
    rows = tm // n_slices
    assert tm % (16 * n_slices) == 0
    mod_spec = lambda tile_of: pl.BlockSpec((None, 3 * N_SUB, D), lambda i, j: (tile_of(i) // per_seq, 0, 0))
    return pl.pallas_call(
        functools.partial(_inproj_kernel, sub=1, tm=tm, tn=tn, rows=rows, n_slices=n_slices),
        grid=(nt, nj),
        in_specs=[
            pl.BlockSpec(memory_space=pl.ANY),
            mod_spec(lambda i: i),
            mod_spec(lambda i: jnp.minimum(i + 1, nt - 1)),
            pl.BlockSpec((1, D), lambda i, j: (0, 0)),
            pl.BlockSpec((None, D, N), lambda i, j: (layer, 0, 0), pipeline_mode=pl.Buffered(1)),
        ],
        out_specs=pl.BlockSpec((tm, tn), lambda i, j: (i, j)),
        out_shape=jax.ShapeDtypeStruct((T, N), F32),
        scratch_shapes=[pltpu.VMEM((tm + rows, D), F32), pltpu.VMEM((tm, D), BF16),
                        pltpu.VMEM((tm + rows, D), BF16), pltpu.SemaphoreType.DMA(())],
        compiler_params=_params("arbitrary", "arbitrary"),
    )(x, mod_l, mod_l, g, w)


def _hgrn_kernel(hq_ref, hf_ref, hi_ref, hg_ref, lb_ref, gn_ref, o_ref, st_ref, *, ts):
    C = HG_CHUNK
    nc = ts // C

    @pl.when(pl.program_id(2) == 0)
    def _():
        st_ref[...] = jnp.zeros_like(st_ref)

    z = hf_ref[...]
    lb = lb_ref[...]
    u = jnp.exp(-jnp.abs(z))
    r = 1.0 / (1.0 + u)
    ur = u * r
    z_pos = z >= 0.0
    log_f = jnp.log(lb + (1.0 - lb) * jnp.where(z_pos, r, ur))
    k = (1.0 - lb) * jnp.where(z_pos, ur, r)
    q = _silu(hq_ref[...])

    rows8 = 8
    per_chunk = C // rows8
    x = log_f.reshape(ts // rows8, rows8, HG_DIM)
    sub = lax.broadcasted_iota(jnp.int32, (1, rows8, HG_DIM), 1)
    step = 1
    while step < rows8:
        x = x + jnp.where(sub >= step, pltpu.roll(x, step, axis=1), 0.0)
        step *= 2
    x = x.reshape(nc, per_chunk, rows8, HG_DIM)
    parts = [x[:, 0]]
    carry = x[:, 0, rows8 - 1:rows8, :]
    for i in range(1, per_chunk):
        parts.append(x[:, i] + carry)
        if i + 1 < per_chunk:
            carry = carry + x[:, i, rows8 - 1:rows8, :]
    b3 = jnp.concatenate(parts, axis=1)

    b_end = b3[:, C - 1:C, :]
    q3 = q.reshape(nc, C, HG_DIM)
    k3 = k.reshape(nc, C, HG_DIM)
    q_in = (q3 * jnp.exp(b3)).astype(BF16).reshape(ts, HG_DIM)
    k_in = (k3 * jnp.exp(-b3)).astype(BF16).reshape(ts, HG_DIM)
    k_out = (k3 * jnp.exp(b_end - b3)).astype(BF16).reshape(ts, HG_DIM)
    decay = jnp.exp(b_end)
    v = hi_ref[...].astype(BF16)

    G = HG_GROUP
    R = G * C
    ri = lax.broadcasted_iota(jnp.int32, (R, R), 0)
    ci = lax.broadcasted_iota(jnp.int32, (R, R), 1)
    causal = (ci <= ri) & ((ri // C) == (ci // C))
    zeros = jnp.zeros((C, HG_DIM), BF16)

    def block_arranged(m):
        return jnp.concatenate(
            [jnp.concatenate([m[c * C:(c + 1) * C] if cc == c else zeros for cc in range(G)], axis=1)
             for c in range(G)], axis=0)

    st = st_ref[...]
    outs = []
    for g in range(ts // R):
        sl = slice(g * R, (g + 1) * R)
        a = jnp.where(causal, _dot_nt(q_in[sl], k_in[sl]), 0.0)
        o = _dot(a.astype(BF16), v[sl])
        kv = _dot_tn(v[sl], block_arranged(k_out[sl]))
        starts = []
        for c in range(G):
            starts.append(st)
            st = st * decay[g * G + c] + kv[:, c * HG_DIM:(c + 1) * HG_DIM]
        st_all = jnp.concatenate(starts, axis=1).astype(BF16)
        outs.append(o + _dot_nt(block_arranged(q_in[sl]), st_all))
    st_ref[...] = st

    o = jnp.concatenate(outs, axis=0)
    o_ref[...] = (_rms(o, gn_ref[...]) * _silu(hg_ref[...])).astype(BF16)


def _hgrn(proj, lb, gn, batch, seq):
    T = proj.shape[0]
    ts = _tile(seq, 2048)
    ns = seq // ts
    H = HG_HEADS

    def col(group):
        return pl.BlockSpec((ts, HG_DIM), lambda b, h, s: (b * ns + s, group * H + h))

    return pl.pallas_call(
        functools.partial(_hgrn_kernel, ts=ts),
        grid=(batch, H, ns),
        in_specs=[col(0), col(1), col(2), col(3),
                  pl.BlockSpec((1, HG_DIM), lambda b, h, s: (0, h)),
                  pl.BlockSpec((1, HG_DIM), lambda b, h, s: (0, 0))],
        out_specs=pl.BlockSpec((ts, HG_DIM), lambda b, h, s: (b * ns + s, h)),
        out_shape=jax.ShapeDtypeStruct((T, H * HG_DIM), BF16),
        scratch_shapes=[pltpu.VMEM((HG_DIM, HG_DIM), F32)],
        compiler_params=_params("parallel", "parallel", "arbitrary"),
    )(proj, proj, proj, proj, lb, gn)


def _mla_proj_kernel(qa_ref, kva_ref, kpe_ref, tab_ref, qg_ref, kg_ref, wq_ref, wk_ref, wv_ref,
                     qf_ref, kf_ref, v_ref, *, tm):
    scale = (MLA_NOPE + MLA_ROPE) ** -0.5
    tab = tab_ref[...]
    low = lax.broadcasted_iota(jnp.int32, (tm, V7X_LANES), 1) < MLA_ROPE

    def rope(y2):
        y = y2 * tab
        return jnp.where(low, y + pltpu.roll(y, MLA_ROPE, axis=1), 0.0)

    cq = _rms(qa_ref[...], qg_ref[...]).astype(BF16)
    qf = _dot(cq, wq_ref[...])
    ckv = _rms(kva_ref[...], kg_ref[...]).astype(BF16)
    kn = _dot(ckv, wk_ref[...])
    v_ref[...] = _dot(ckv, wv_ref[...]).astype(BF16)
    pe = kpe_ref[...]
    lane = lax.broadcasted_iota(jnp.int32, (tm, V7X_LANES), 1)
    half = MLA_ROPE // 2
    pe2 = jnp.where(low, pe, jnp.where(lane < MLA_ROPE + half, -pltpu.roll(pe, half, axis=1),
                                       pltpu.roll(pe, V7X_LANES - half, axis=1)))
    kr = rope(pe2).astype(BF16)
    P = MLA_QK_PAD
    for h in range(MLA_HEADS):
        qf_ref[:, h * P:h * P + MLA_NOPE] = (qf[:, h * P:h * P + MLA_NOPE] * scale).astype(BF16)
        qf_ref[:, h * P + MLA_NOPE:(h + 1) * P] = (rope(qf[:, h * P + MLA_NOPE:(h + 1) * P]) * scale).astype(BF16)
        kf_ref[:, h * P:h * P + MLA_NOPE] = kn[:, h * MLA_NOPE:(h + 1) * MLA_NOPE].astype(BF16)
        kf_ref[:, h * P + MLA_NOPE:(h + 1) * P] = kr


def _mla_proj(proj, tab, qg, kg, wq, wk, wv, seq):
    T = proj.shape[0]
    tm = _tile(seq, 512)
    qa_col = (4 * HG_HEADS * HG_DIM) // MLA_Q_RANK
    kva_col = (4 * HG_HEADS * HG_DIM + MLA_Q_RANK) // MLA_KV_RANK
    kpe_col = (4 * HG_HEADS * HG_DIM + MLA_Q_RANK + MLA_KV_RANK) // V7X_LANES
    full = lambda a: pl.BlockSpec(a.shape, lambda i: (0, 0))
    H = MLA_HEADS
    return pl.pallas_call(
        functools.partial(_mla_proj_kernel, tm=tm),
        grid=(T // tm,),
        in_specs=[
            pl.BlockSpec((tm, MLA_Q_RANK), lambda i: (i, qa_col)),
            pl.BlockSpec((tm, MLA_KV_RANK), lambda i: (i, kva_col)),
            pl.BlockSpec((tm, V7X_LANES), lambda i: (i, kpe_col)),
            pl.BlockSpec((tm, V7X_LANES), lambda i: (i, 0)),
            full(qg), full(kg), full(wq), full(wk), full(wv),
        ],
        out_specs=[
            pl.BlockSpec((tm, H * MLA_QK_PAD), lambda i: (i, 0)),
            pl.BlockSpec((tm, H * MLA_QK_PAD), lambda i: (i, 0)),
            pl.BlockSpec((tm, H * MLA_V), lambda i: (i, 0)),
        ],
        out_shape=[
            jax.ShapeDtypeStruct((T, H * MLA_QK_PAD), BF16),
            jax.ShapeDtypeStruct((T, H * MLA_QK_PAD), BF16),
            jax.ShapeDtypeStruct((T, H * MLA_V), BF16),
        ],
        compiler_params=_params("parallel"),
    )(proj, proj, proj, tab, qg, kg, wq, wk, wv)


def _attn_kernel(q_ref, k_ref, v_ref, o_ref, *, seq, t):
    n = seq // t
    keep = (lax.broadcasted_iota(jnp.int32, (t, t), 1)
            <= lax.broadcasted_iota(jnp.int32, (t, t), 0))

    def update(state, s, v):
        row_max = jnp.max(s, axis=-1, keepdims=True)
        if state is None:
            p = jnp.exp(s - row_max)
            return row_max, jnp.sum(p, axis=-1, keepdims=True), _dot(p.astype(BF16), v)
        m, l, acc = state
        m_new = jnp.maximum(m, row_max)
        alpha = jnp.exp(m - m_new)
        p = jnp.exp(s - m_new)
        return (m_new, alpha * l + jnp.sum(p, axis=-1, keepdims=True),
                alpha * acc + _dot(p.astype(BF16), v))

    for i in range(n):
        q = q_ref[i * t:(i + 1) * t, :]
        state = None
        for j in range(i + 1):
            s = _dot_nt(q, k_ref[j * t:(j + 1) * t, :])
            if j == i:
                s = jnp.where(keep, s, -jnp.inf)
            state = update(state, s, v_ref[j * t:(j + 1) * t, :])
        _, l, acc = state
        o_ref[i * t:(i + 1) * t, :] = (acc / l).astype(BF16)


def _attn(qf, kf, v, batch, seq):
    T = qf.shape[0]
    t = _tile(seq, 512)
    H = MLA_HEADS
    return pl.pallas_call(
        functools.partial(_attn_kernel, seq=seq, t=t),
        grid=(batch, H),
        in_specs=[
            pl.BlockSpec((seq, MLA_QK_PAD), lambda b, h: (b, h)),
            pl.BlockSpec((seq, MLA_QK_PAD), lambda b, h: (b, h)),
            pl.BlockSpec((seq, MLA_V), lambda b, h: (b, h)),
        ],
        out_specs=pl.BlockSpec((seq, MLA_V), lambda b, h: (b, h)),
        out_shape=jax.ShapeDtypeStruct((T, H * MLA_V), BF16),
        compiler_params=_params("parallel", "parallel"),
    )(qf, kf, v)


def _outproj_kernel(x_ref, mod_ref, a_ref, b_ref, wa_ref, wb_ref, o_ref, *, sub):
    y = _dot(a_ref[...], wa_ref[...]) + _dot(b_ref[...], wb_ref[...])
    o_ref[...] = x_ref[...] + mod_ref[3 * sub + 2:3 * sub + 3, :] * y


def _outproj(x, mod_l, a, b, wa, wb, seq):
    T, D = x.shape
    tm = _tile(seq, 512)
    per_seq = seq // tm
    return pl.pallas_call(
        functools.partial(_outproj_kernel, sub=1),
        grid=(T // tm,),
        in_specs=[
            pl.BlockSpec((tm, D), lambda i: (i, 0)),
            pl.BlockSpec((None, 3 * N_SUB, D), lambda i: (i // per_seq, 0, 0)),
            pl.BlockSpec((tm, a.shape[1]), lambda i: (i, 0)),
            pl.BlockSpec((tm, b.shape[1]), lambda i: (i, 0)),
            pl.BlockSpec(wa.shape, lambda i: (0, 0)),
            pl.BlockSpec(wb.shape, lambda i: (0, 0)),
        ],
        out_specs=pl.BlockSpec((tm, D), lambda i: (i, 0)),
        out_shape=jax.ShapeDtypeStruct((T, D), F32),
        compiler_params=_params("parallel"),
    )(x, mod_l, a, b, wa, wb)


def _rotate_half_cols(w):
    half = MLA_ROPE // 2
    w1, w2 = w[..., :half], w[..., half:]
    return jnp.concatenate([-w2, w1], axis=-1)


def _prep_w_in(w_in):
    pad = (-w_in.shape[-1]) % (2 * V7X_MXU_DIM)
    return jnp.pad(w_in.astype(BF16), ((0, 0), (0, 0), (0, pad)))


def _prep_w_q(w_q_up):
    L, R, _ = w_q_up.shape
    w = w_q_up.reshape(L, R, MLA_HEADS, MLA_NOPE + MLA_ROPE)
    pe = w[..., MLA_NOPE:]
    w = jnp.concatenate([w, _rotate_half_cols(pe)], axis=-1)
    return w.reshape(L, R, MLA_HEADS * MLA_QK_PAD).astype(BF16)


def _prep_w_kv(w_kv_up):
    L, R, _ = w_kv_up.shape
    w = w_kv_up.reshape(L, R, MLA_HEADS, MLA_NOPE + MLA_V)
    wk = w[..., :MLA_NOPE].reshape(L, R, MLA_HEADS * MLA_NOPE)
    wv = w[..., MLA_NOPE:].reshape(L, R, MLA_HEADS * MLA_V)
    return wk.astype(BF16), wv.astype(BF16)


def kernel(x, c, positions, w_ada, b_ada, norm_g, w_in, qa_norm_g, w_q_up, kva_norm_g, w_kv_up,
           hg_lb_logits, hg_norm_g, w_out, ffn_w_gate, ffn_w_up, ffn_w_down, final_norm_g):
    B, S, D = x.shape
    L = w_ada.shape[0]
    T = B * S

    half = MLA_ROPE // 2
    inv_freq = ROPE_THETA ** (-jnp.arange(half, dtype=F32) / half)
    ang = positions.astype(F32).reshape(T, 1) * inv_freq
    cos, sin = jnp.cos(ang), jnp.sin(ang)
    tab = jnp.concatenate([cos, cos, sin, sin], axis=-1)

    lb_all = jnp.cumsum(jax.nn.softmax(hg_lb_logits.astype(F32), axis=0), axis=0)
    lb_all = lb_all - lb_all[0:1]

    w_in_p = _prep_w_in(w_in)
    w_q_p = _prep_w_q(w_q_up)
    w_k_p, w_v_p = _prep_w_kv(w_kv_up)
    hg_w = HG_HEADS * HG_DIM
    w_out_a = w_out[:, :hg_w, :].astype(BF16)
    w_out_b = w_out[:, hg_w:, :].astype(BF16)
    wg, wu, wd = ffn_w_gate.astype(BF16), ffn_w_up.astype(BF16), ffn_w_down.astype(BF16)

    mod = _ada(c, w_ada, b_ada).reshape(L, B, 3 * N_SUB, D)

    fin_g = final_norm_g.reshape(1, D)
    xf = x.reshape(T, D)
    for l in range(L):
        mod_l = mod[l]
        xf = _ffn(xf, mod_l, norm_g[l, 0:1], wg, wu, wd, fin_g, l, 0, 0, S, False)
        proj = _inproj(xf, mod_l, norm_g[l, 1:2], w_in_p, l, S)
        o_hg = _hgrn(proj, lb_all[l:l + 1], hg_norm_g[l:l + 1], B, S)
        qf, kf, v = _mla_proj(proj, tab, qa_norm_g[l:l + 1], kva_norm_g[l:l + 1],
                              w_q_p[l], w_k_p[l], w_v_p[l], S)
        o_mla = _attn(qf, kf, v, B, S)
        xf = _outproj(xf, mod_l, o_hg, o_mla, w_out_a[l], w_out_b[l], S)
        xf = _ffn(xf, mod_l, norm_g[l, 2:3], wg, wu, wd, fin_g, l, 1, 2, S, l == L - 1)
    return xf.reshape(B, S, D)
```

```python
import functools

import jax
import jax.numpy as jnp
from jax import lax
from jax.experimental import pallas as pl
from jax.experimental.pallas import tpu as pltpu

EPS = 1e-6
N_SUB = 3
HG_HEADS = 8
HG_DIM = 128
HG_CHUNK = 32
HG_GROUP = 4
HG_HEADS_PER_STEP = 4
HG_ROW_BLOCK = 512
MLA_HEADS = 8
MLA_NOPE = 128
MLA_ROPE = 64
MLA_V = 128
MLA_Q_RANK = 512
MLA_KV_RANK = 256
ROPE_THETA = 10000.0
MLA_QK_PAD = 256
FFN_DOWN_SLAB = 512
FFN_TM = 1024

V7X_LANES = 128
V7X_MXU_DIM = 256
V7X_VMEM_BYTES = 64 * 1024 * 1024
VMEM_LIMIT = V7X_VMEM_BYTES - 3 * 1024 * 1024

F32 = jnp.float32
BF16 = jnp.bfloat16


def _params(*sem):
    return pltpu.CompilerParams(dimension_semantics=sem, vmem_limit_bytes=VMEM_LIMIT)


def _sigmoid(x):
    return 1.0 / (1.0 + jnp.exp(-x))


def _silu(x):
    return x * _sigmoid(x)


def _rms(x, g):
    ms = jnp.mean(x * x, axis=-1, keepdims=True)
    return (x * lax.rsqrt(ms + EPS)) * g


def _modulated_norm(x, g, shift, scale):
    return _rms(x, g) * (1.0 + scale) + shift


def _dot(a, b):
    return jnp.dot(a, b, preferred_element_type=F32)


def _dot_nt(a, b):
    return lax.dot_general(a, b, (((1,), (1,)), ((), ())), preferred_element_type=F32)


def _dot_tn(a, b):
    return lax.dot_general(a, b, (((0,), (0,)), ((), ())), preferred_element_type=F32)


def _tile(n, pref):
    t = min(n, pref)
    assert n % t == 0, (n, t)
    return t


def _ada_kernel(c_ref, w_ref, b_ref, o_ref):
    ca = _silu(c_ref[...]).astype(BF16)
    o_ref[...] = _dot(ca, w_ref[...].astype(BF16)) + b_ref[...]


def _ada(c, w_ada, b_ada):
    L, D, N = w_ada.shape
    B = c.shape[0]
    tn = _tile(N, 1024)
    return pl.pallas_call(
        _ada_kernel,
        grid=(L, N // tn),
        in_specs=[
            pl.BlockSpec((B, D), lambda l, j: (0, 0)),
            pl.BlockSpec((None, D, tn), lambda l, j: (l, 0, j)),
            pl.BlockSpec((None, 1, tn), lambda l, j: (l, 0, j)),
        ],
        out_specs=pl.BlockSpec((None, B, tn), lambda l, j: (l, 0, j)),
        out_shape=jax.ShapeDtypeStruct((L, B, N), F32),
        compiler_params=_params("parallel", "parallel"),
    )(c, w_ada, b_ada.reshape(L, 1, N))


def _ffn_kernel(x_hbm, mod_ref, modn_ref, g_ref, wg_ref, wu_ref, wd_ref, fin_ref, o_ref,
                xbuf, h_cur, h_next, sem, *, sub, tm, rows, n_slices, final):
    i = pl.program_id(0)
    j = pl.program_id(1)
    has_next = i + 1 < pl.num_programs(0)
    shift, scale, gate_row = 3 * sub, 3 * sub + 1, 3 * sub + 2

    def fetch(tile):
        return pltpu.make_async_copy(x_hbm.at[pl.ds(tile * tm, tm), :], xbuf.at[pl.ds(0, tm), :], sem)

    def norm_rows(r0, nrows, mod_src):
        h = _modulated_norm(xbuf[pl.ds(r0, nrows), :], g_ref[...],
                            mod_src[shift:shift + 1, :], mod_src[scale:scale + 1, :])
        h_next[pl.ds(r0, nrows), :] = h.astype(BF16)
        return h

    @pl.when((i == 0) & (j == 0))
    def _():
        first = fetch(0)
        first.start()
        xbuf[pl.ds(tm, rows), :] = jnp.zeros((rows, xbuf.shape[1]), F32)
        first.wait()
        for s in range(n_slices):
            norm_rows(s * rows, rows, mod_ref)

    @pl.when(j == 0)
    def _():
        o_ref[...] = xbuf[pl.ds(0, tm), :]
        h_cur[...] = h_next[pl.ds(0, tm), :]

    @pl.when((j == 2) & has_next)
    def _():
        fetch(i + 1).wait()

    scheduled = (j >= 2) & (j < 2 + n_slices)
    r0 = pl.multiple_of(jnp.where(scheduled, (j - 2) * rows, tm), rows)
    hn = norm_rows(r0, rows, modn_ref)
    d_model = o_ref.shape[1]
    tf = wg_ref.shape[1]
    bits = pltpu.bitcast(hn, jnp.uint32).reshape(rows // 8, 8, d_model)
    folded = bits[0]
    for r in range(1, rows // 8):
        folded = folded | bits[r]
    lanes = folded[:, 0:tf]
    for k in range(1, d_model // tf):
        lanes = lanes | folded[:, k * tf:(k + 1) * tf]
    half_word = jnp.uint32(16)
    zero = pltpu.bitcast(lax.shift_right_logical(lax.shift_right_logical(lanes, half_word), half_word), F32)

    h = h_cur[...]
    gate = _dot(h, wg_ref[...])
    up = (_dot(h, wu_ref[...]).reshape(tm // 8, 8, tf) + zero).reshape(tm, tf)
    a = (_silu(gate) * up).astype(BF16)
    c = 0.5 * mod_ref[gate_row:gate_row + 1, :]
    slab = min(d_model, FFN_DOWN_SLAB)
    for n in range(d_model // slab):
        sl = slice(n * slab, (n + 1) * slab)
        o_ref[:, sl] += c[:, sl] * _dot(a, wd_ref[:, sl])

    @pl.when((j == 0) & has_next)
    def _():
        fetch(i + 1).start()

    if final:
        @pl.when(j == pl.num_programs(1) - 1)
        def _():
            o_ref[...] = _rms(o_ref[...], fin_ref[...])


def _ffn(x, mod_l, g, wg, wu, wd, fin_g, layer, which, sub, seq, final):
    T, D = x.shape
    F = wg.shape[-1]
    tm = _tile(seq, FFN_TM)
    tf = _tile(F, 512)
    nt, nf = T // tm, F // tf
    per_seq = seq // tm
    n_slices = 1
    while 2 * n_slices <= min(nf - 2, 8):
        n_slices *= 2
    assert nf >= 2 + n_slices and tm % (16 * n_slices) == 0, (nf, tm)
    rows = tm // n_slices
    mod_spec = lambda tile_of: pl.BlockSpec((None, 3 * N_SUB, D), lambda i, j: (tile_of(i) // per_seq, 0, 0))
    return pl.pallas_call(
        functools.partial(_ffn_kernel, sub=sub, tm=tm, rows=rows, n_slices=n_slices, final=final),
        grid=(nt, nf),
        in_specs=[
            pl.BlockSpec(memory_space=pl.ANY),
            mod_spec(lambda i: i),
            mod_spec(lambda i: jnp.minimum(i + 1, nt - 1)),
            pl.BlockSpec((1, D), lambda i, j: (0, 0)),
            pl.BlockSpec((None, None, D, tf), lambda i, j: (layer, which, 0, j)),
            pl.BlockSpec((None, None, D, tf), lambda i, j: (layer, which, 0, j)),
            pl.BlockSpec((None, None, tf, D), lambda i, j: (layer, which, j, 0)),
            pl.BlockSpec((1, D), lambda i, j: (0, 0)),
        ],
        out_specs=pl.BlockSpec((tm, D), lambda i, j: (i, 0)),
        out_shape=jax.ShapeDtypeStruct((T, D), F32),
        scratch_shapes=[pltpu.VMEM((tm + rows, D), F32), pltpu.VMEM((tm, D), BF16),
                        pltpu.VMEM((tm + rows, D), BF16), pltpu.SemaphoreType.DMA(())],
        compiler_params=_params("arbitrary", "arbitrary"),
    )(x, mod_l, mod_l, g, wg, wu, wd, fin_g)


def _inproj_kernel(x_ref, mod_ref, g_ref, w_ref, h_ref, o_ref, *, sub):
    h = _modulated_norm(x_ref[...], g_ref[...], mod_ref[3 * sub:3 * sub + 1, :],
                        mod_ref[3 * sub + 1:3 * sub + 2, :]).astype(BF16)
    h_ref[...] = h
    o_ref[...] = _dot(h, w_ref[...])


def _inproj(x, mod_l, g, w, layer, seq):
    T, D = x.shape
    N = w.shape[-1]
    tm = _tile(seq, 512)
    per_seq = seq // tm
    return pl.pallas_call(
        functools.partial(_inproj_kernel, sub=1),
        grid=(T // tm,),
        in_specs=[
            pl.BlockSpec((tm, D), lambda i: (i, 0)),
            pl.BlockSpec((None, 3 * N_SUB, D), lambda i: (i // per_seq, 0, 0)),
            pl.BlockSpec((1, D), lambda i: (0, 0)),
            pl.BlockSpec((None, D, N), lambda i: (layer, 0, 0)),
        ],
        out_specs=[pl.BlockSpec((tm, D), lambda i: (i, 0)), pl.BlockSpec((tm, N), lambda i: (i, 0))],
        out_shape=[jax.ShapeDtypeStruct((T, D), BF16), jax.ShapeDtypeStruct((T, N), F32)],
        compiler_params=_params("parallel"),
    )(x, mod_l, g, w)


def _hgrn_head(hq, z, hi, hg, lb, gn, st, *, ts):
    C = HG_CHUNK
    nc = ts // C
    u = jnp.exp(-jnp.abs(z))
    r = 1.0 / (1.0 + u)
    ur = u * r
    z_pos = z >= 0.0
    log_f = jnp.log(lb + (1.0 - lb) * jnp.where(z_pos, r, ur))
    k = (1.0 - lb) * jnp.where(z_pos, ur, r)
    q = _silu(hq)

    rows8 = 8
    per_chunk = C // rows8
    x = log_f.reshape(ts // rows8, rows8, HG_DIM)
    sub = lax.broadcasted_iota(jnp.int32, (1, rows8, HG_DIM), 1)
    step = 1
    while step < rows8:
        x = x + jnp.where(sub >= step, pltpu.roll(x, step, axis=1), 0.0)
        step *= 2
    x = x.reshape(nc, per_chunk, rows8, HG_DIM)
    parts = [x[:, 0]]
    carry = x[:, 0, rows8 - 1:rows8, :]
    for i in range(1, per_chunk):
        parts.append(x[:, i] + carry)
        if i + 1 < per_chunk:
            carry = carry + x[:, i, rows8 - 1:rows8, :]
    b3 = jnp.concatenate(parts, axis=1)

    b_end = b3[:, C - 1:C, :]
    q3 = q.reshape(nc, C, HG_DIM)
    k3 = k.reshape(nc, C, HG_DIM)
    decay = jnp.exp(b_end)
    q_in = (q3 * jnp.exp(b3)).astype(BF16).reshape(ts, HG_DIM)
    k_back = k3 * jnp.exp(-b3)
    k_in = k_back.astype(BF16).reshape(ts, HG_DIM)
    k_out = (k_back * decay).astype(BF16).reshape(ts, HG_DIM)
    v = hi.astype(BF16)

    G = HG_GROUP
    R = G * C
    ri = lax.broadcasted_iota(jnp.int32, (R, R), 0)
    ci = lax.broadcasted_iota(jnp.int32, (R, R), 1)
    causal = (ci <= ri) & ((ri // C) == (ci // C))
    zeros = jnp.zeros((C, HG_DIM), BF16)

    def block_arranged(m):
        return jnp.concatenate(
            [jnp.concatenate([m[c * C:(c + 1) * C] if cc == c else zeros for cc in range(G)], axis=1)
             for c in range(G)], axis=0)

    outs = []
    for g in range(ts // R):
        sl = slice(g * R, (g + 1) * R)
        a = jnp.where(causal, _dot_nt(q_in[sl], k_in[sl]), 0.0)
        o = _dot(a.astype(BF16), v[sl])
        kv = _dot_tn(v[sl], block_arranged(k_out[sl]))
        starts = []
        for c in range(G):
            starts.append(st)
            st = st * decay[g * G + c] + kv[:, c * HG_DIM:(c + 1) * HG_DIM]
        st_all = jnp.concatenate(starts, axis=1).astype(BF16)
        outs.append(o + _dot_nt(block_arranged(q_in[sl]), st_all))
    o = jnp.concatenate(outs, axis=0)
    return (_rms(o, gn) * _silu(hg)).astype(BF16), st


def _hgrn_kernel(h_ref, w_ref, lb_ref, gn_ref, o_ref, st_ref, *, ts, heads):
    @pl.when(pl.program_id(2) == 0)
    def _():
        st_ref[...] = jnp.zeros_like(st_ref)

    rb = min(ts, HG_ROW_BLOCK)
    states = [st_ref[h] for h in range(heads)]
    for r in range(ts // rb):
        rows = slice(r * rb, (r + 1) * rb)
        y = _dot(h_ref[rows, :], w_ref[...])
        for h in range(heads):
            hq, z, hi, hg = (y[:, (4 * h + g) * HG_DIM:(4 * h + g + 1) * HG_DIM] for g in range(4))
            sl = slice(h * HG_DIM, (h + 1) * HG_DIM)
            o, states[h] = _hgrn_head(hq, z, hi, hg, lb_ref[:, sl], gn_ref[...], states[h], ts=rb)
            o_ref[rows, sl] = o
    for h in range(heads):
        st_ref[h] = states[h]


def _hgrn(h, w, lb, gn, layer, batch, seq):
    T, D = h.shape
    ts = _tile(seq, 2048)
    ns = seq // ts
    hp = HG_HEADS_PER_STEP
    nh = HG_HEADS // hp
    wo = hp * HG_DIM
    return pl.pallas_call(
        functools.partial(_hgrn_kernel, ts=ts, heads=hp),
        grid=(batch, nh, ns),
        in_specs=[pl.BlockSpec((ts, D), lambda b, p, s: (b * ns + s, 0)),
                  pl.BlockSpec((None, None, D, 4 * wo), lambda b, p, s: (layer, p, 0, 0)),
                  pl.BlockSpec((1, wo), lambda b, p, s: (0, p)),
                  pl.BlockSpec((1, HG_DIM), lambda b, p, s: (0, 0))],
        out_specs=pl.BlockSpec((ts, wo), lambda b, p, s: (b * ns + s, p)),
        out_shape=jax.ShapeDtypeStruct((T, HG_HEADS * HG_DIM), BF16),
        scratch_shapes=[pltpu.VMEM((hp, HG_DIM, HG_DIM), F32)],
        compiler_params=_params("parallel", "arbitrary", "arbitrary"),
    )(h, w, lb, gn)


def _mla_proj_kernel(qa_ref, kva_ref, kpe_ref, tab_ref, qg_ref, kg_ref, wq_ref, wk_ref, wv_ref,
                     qf_ref, kf_ref, v_ref, *, tm):
    scale = (MLA_NOPE + MLA_ROPE) ** -0.5
    tab = tab_ref[...]
    low = lax.broadcasted_iota(jnp.int32, (tm, V7X_LANES), 1) < MLA_ROPE

    def rope(y2):
        y = y2 * tab
        return jnp.where(low, y + pltpu.roll(y, MLA_ROPE, axis=1), 0.0)

    cq = _rms(qa_ref[...], qg_ref[...]).astype(BF16)
    qf = _dot(cq, wq_ref[...])
    ckv = _rms(kva_ref[...], kg_ref[...]).astype(BF16)
    kn = _dot(ckv, wk_ref[...])
    v_ref[...] = _dot(ckv, wv_ref[...]).astype(BF16)
    pe = kpe_ref[...]
    lane = lax.broadcasted_iota(jnp.int32, (tm, V7X_LANES), 1)
    half = MLA_ROPE // 2
    pe2 = jnp.where(low, pe, jnp.where(lane < MLA_ROPE + half, -pltpu.roll(pe, half, axis=1),
                                       pltpu.roll(pe, V7X_LANES - half, axis=1)))
    kr = rope(pe2).astype(BF16)
    P = MLA_QK_PAD
    for h in range(MLA_HEADS):
        qf_ref[:, h * P:h * P + MLA_NOPE] = (qf[:, h * P:h * P + MLA_NOPE] * scale).astype(BF16)
        qf_ref[:, h * P + MLA_NOPE:(h + 1) * P] = (rope(qf[:, h * P + MLA_NOPE:(h + 1) * P]) * scale).astype(BF16)
        kf_ref[:, h * P:h * P + MLA_NOPE] = kn[:, h * MLA_NOPE:(h + 1) * MLA_NOPE].astype(BF16)
        kf_ref[:, h * P + MLA_NOPE:(h + 1) * P] = kr


def _mla_proj(proj, tab, qg, kg, wq, wk, wv, seq):
    T = proj.shape[0]
    tm = _tile(seq, 512)
    qa_col = 0
    kva_col = MLA_Q_RANK // MLA_KV_RANK
    kpe_col = (MLA_Q_RANK + MLA_KV_RANK) // V7X_LANES
    full = lambda a: pl.BlockSpec(a.shape, lambda i: (0, 0))
    H = MLA_HEADS
    return pl.pallas_call(
        functools.partial(_mla_proj_kernel, tm=tm),
        grid=(T // tm,),
        in_specs=[
            pl.BlockSpec((tm, MLA_Q_RANK), lambda i: (i, qa_col)),
            pl.BlockSpec((tm, MLA_KV_RANK), lambda i: (i, kva_col)),
            pl.BlockSpec((tm, V7X_LANES), lambda i: (i, kpe_col)),
            pl.BlockSpec((tm, V7X_LANES), lambda i: (i, 0)),
            full(qg), full(kg), full(wq), full(wk), full(wv),
        ],
        out_specs=[
            pl.BlockSpec((tm, H * MLA_QK_PAD), lambda i: (i, 0)),
            pl.BlockSpec((tm, H * MLA_QK_PAD), lambda i: (i, 0)),
            pl.BlockSpec((tm, H * MLA_V), lambda i: (i, 0)),
        ],
        out_shape=[
            jax.ShapeDtypeStruct((T, H * MLA_QK_PAD), BF16),
            jax.ShapeDtypeStruct((T, H * MLA_QK_PAD), BF16),
            jax.ShapeDtypeStruct((T, H * MLA_V), BF16),
        ],
        compiler_params=_params("parallel"),
    )(proj, proj, proj, tab, qg, kg, wq, wk, wv)


def _attn_kernel(q_ref, k_ref, v_ref, o_ref, *, seq, t):
    n = seq // t
    keep = (lax.broadcasted_iota(jnp.int32, (t, t), 1)
            <= lax.broadcasted_iota(jnp.int32, (t, t), 0))

    def update(state, s, v):
        row_max = jnp.max(s, axis=-1, keepdims=True)
        if state is None:
            p = jnp.exp(s - row_max)
            return row_max, jnp.sum(p, axis=-1, keepdims=True), _dot(p.astype(BF16), v)
        m, l, acc = state
        m_new = jnp.maximum(m, row_max)
        alpha = jnp.exp(m - m_new)
        p = jnp.exp(s - m_new)
        return (m_new, alpha * l + jnp.sum(p, axis=-1, keepdims=True),
                alpha * acc + _dot(p.astype(BF16), v))

    for i in range(n):
        q = q_ref[i * t:(i + 1) * t, :]
        state = None
        for j in range(i + 1):
            s = _dot_nt(q, k_ref[j * t:(j + 1) * t, :])
            if j == i:
                s = jnp.where(keep, s, -jnp.inf)
            state = update(state, s, v_ref[j * t:(j + 1) * t, :])
        _, l, acc = state
        o_ref[i * t:(i + 1) * t, :] = (acc / l).astype(BF16)


def _attn(qf, kf, v, batch, seq):
    T = qf.shape[0]
    t = _tile(seq, 512)
    H = MLA_HEADS
    return pl.pallas_call(
        functools.partial(_attn_kernel, seq=seq, t=t),
        grid=(batch, H),
        in_specs=[
            pl.BlockSpec((seq, MLA_QK_PAD), lambda b, h: (b, h)),
            pl.BlockSpec((seq, MLA_QK_PAD), lambda b, h: (b, h)),
            pl.BlockSpec((seq, MLA_V), lambda b, h: (b, h)),
        ],
        out_specs=pl.BlockSpec((seq, MLA_V), lambda b, h: (b, h)),
        out_shape=jax.ShapeDtypeStruct((T, H * MLA_V), BF16),
        compiler_params=_params("parallel", "parallel"),
    )(qf, kf, v)


def _outproj_kernel(x_ref, mod_ref, a_ref, b_ref, wa_ref, wb_ref, o_ref, *, sub):
    y = _dot(a_ref[...], wa_ref[...]) + _dot(b_ref[...], wb_ref[...])
    o_ref[...] = x_ref[...] + mod_ref[3 * sub + 2:3 * sub + 3, :] * y


def _outproj(x, mod_l, a, b, wa, wb, seq):
    T, D = x.shape
    tm = _tile(seq, 512)
    per_seq = seq // tm
    return pl.pallas_call(
        functools.partial(_outproj_kernel, sub=1),
        grid=(T // tm,),
        in_specs=[
            pl.BlockSpec((tm, D), lambda i: (i, 0)),
            pl.BlockSpec((None, 3 * N_SUB, D), lambda i: (i // per_seq, 0, 0)),
            pl.BlockSpec((tm, a.shape[1]), lambda i: (i, 0)),
            pl.BlockSpec((tm, b.shape[1]), lambda i: (i, 0)),
            pl.BlockSpec(wa.shape, lambda i: (0, 0)),
            pl.BlockSpec(wb.shape, lambda i: (0, 0)),
        ],
        out_specs=pl.BlockSpec((tm, D), lambda i: (i, 0)),
        out_shape=jax.ShapeDtypeStruct((T, D), F32),
        compiler_params=_params("parallel"),
    )(x, mod_l, a, b, wa, wb)


def _rotate_half_cols(w):
    half = MLA_ROPE // 2
    w1, w2 = w[..., :half], w[..., half:]
    return jnp.concatenate([-w2, w1], axis=-1)


def _prep_w_in(w_in):
    L, D, _ = w_in.shape
    hp, d = HG_HEADS_PER_STEP, HG_DIM
    nh = HG_HEADS // hp
    n_hg = 4 * HG_HEADS * d
    w_hg = w_in[:, :, :n_hg].astype(BF16).reshape(L, D, 4, nh, hp, d)
    w_hg = w_hg.transpose(0, 3, 1, 4, 2, 5).reshape(L, nh, D, hp * 4 * d)
    w_mla = w_in[:, :, n_hg:].astype(BF16)
    pad = (-w_mla.shape[-1]) % V7X_MXU_DIM
    return w_hg, jnp.pad(w_mla, ((0, 0), (0, 0), (0, pad)))


def _prep_w_q(w_q_up):
    L, R, _ = w_q_up.shape
    w = w_q_up.reshape(L, R, MLA_HEADS, MLA_NOPE + MLA_ROPE)
    pe = w[..., MLA_NOPE:]
    w = jnp.concatenate([w, _rotate_half_cols(pe)], axis=-1)
    return w.reshape(L, R, MLA_HEADS * MLA_QK_PAD).astype(BF16)


def _prep_w_kv(w_kv_up):
    L, R, _ = w_kv_up.shape
    w = w_kv_up.reshape(L, R, MLA_HEADS, MLA_NOPE + MLA_V)
    wk = w[..., :MLA_NOPE].reshape(L, R, MLA_HEADS * MLA_NOPE)
    wv = w[..., MLA_NOPE:].reshape(L, R, MLA_HEADS * MLA_V)
    return wk.astype(BF16), wv.astype(BF16)


def kernel(x, c, positions, w_ada, b_ada, norm_g, w_in, qa_norm_g, w_q_up, kva_norm_g, w_kv_up,
           hg_lb_logits, hg_norm_g, w_out, ffn_w_gate, ffn_w_up, ffn_w_down, final_norm_g):
    B, S, D = x.shape
    L = w_ada.shape[0]
    T = B * S

    half = MLA_ROPE // 2
    inv_freq = ROPE_THETA ** (-jnp.arange(half, dtype=F32) / half)
    ang = positions.astype(F32).reshape(T, 1) * inv_freq
    cos, sin = jnp.cos(ang), jnp.sin(ang)
    tab = jnp.concatenate([cos, cos, sin, sin], axis=-1)

    lb_all = jnp.cumsum(jax.nn.softmax(hg_lb_logits.astype(F32), axis=0), axis=0)
    lb_all = lb_all - lb_all[0:1]

    w_hg_p, w_mla_p = _prep_w_in(w_in)
    w_q_p = _prep_w_q(w_q_up)
    w_k_p, w_v_p = _prep_w_kv(w_kv_up)
    hg_w = HG_HEADS * HG_DIM
    w_out_a = w_out[:, :hg_w, :].astype(BF16)
    w_out_b = w_out[:, hg_w:, :].astype(BF16)
    wg, wu, wd = ffn_w_gate.astype(BF16), ffn_w_up.astype(BF16), ffn_w_down.astype(BF16)

    mod = _ada(c, w_ada, b_ada).reshape(L, B, 3 * N_SUB, D)

    fin_g = final_norm_g.reshape(1, D)
    xf = x.reshape(T, D)
    for l in range(L):
        mod_l = mod[l]
        xf = _ffn(xf, mod_l, norm_g[l, 0:1], wg, wu, wd, fin_g, l, 0, 0, S, False)
        h, proj = _inproj(xf, mod_l, norm_g[l, 1:2], w_mla_p, l, S)
        o_hg = _hgrn(h, w_hg_p, lb_all[l:l + 1], hg_norm_g[l:l + 1], l, B, S)
        qf, kf, v = _mla_proj(proj, tab, qa_norm_g[l:l + 1], kva_norm_g[l:l + 1],
                              w_q_p[l], w_k_p[l], w_v_p[l], S)
        o_mla = _attn(qf, kf, v, B, S)
        xf = _outproj(xf, mod_l, o_hg, o_mla, w_out_a[l], w_out_b[l], S)
        xf = _ffn(xf, mod_l, norm_g[l, 2:3], wg, wu, wd, fin_g, l, 1, 2, S, l == L - 1)
    return xf.reshape(B, S, D)
```

```python
import functools

import jax
import jax.numpy as jnp
from jax import lax
from jax.experimental import pallas as pl
from jax.experimental.pallas import tpu as pltpu

EPS = 1e-6
N_SUB = 3
HG_HEADS = 8
HG_DIM = 128
HG_CHUNK = 32
HG_GROUP = 4
HG_HEADS_PER_STEP = 4
HG_ROW_BLOCK = 512
MLA_HEADS = 8
MLA_NOPE = 128
MLA_ROPE = 64
MLA_V = 128
MLA_Q_RANK = 512
MLA_KV_RANK = 256
ROPE_THETA = 10000.0
MLA_QK_PAD = 256
FFN_DOWN_SLAB = 512
FFN_TM = 1024

V7X_LANES = 128
V7X_MXU_DIM = 256
V7X_VMEM_BYTES = 64 * 1024 * 1024
VMEM_LIMIT = V7X_VMEM_BYTES - 3 * 1024 * 1024

F32 = jnp.float32
BF16 = jnp.bfloat16


def _params(*sem):
    return pltpu.CompilerParams(dimension_semantics=sem, vmem_limit_bytes=VMEM_LIMIT)


def _sigmoid(x):
    return 1.0 / (1.0 + jnp.exp(-x))


def _silu(x):
    return x * _sigmoid(x)


def _rms(x, g):
    ms = jnp.mean(x * x, axis=-1, keepdims=True)
    return (x * lax.rsqrt(ms + EPS)) * g


def _modulated_norm(x, g, shift, scale):
    return _rms(x, g) * (1.0 + scale) + shift


def _dot(a, b):
    return jnp.dot(a, b, preferred_element_type=F32)


def _dot_nt(a, b):
    return lax.dot_general(a, b, (((1,), (1,)), ((), ())), preferred_element_type=F32)


def _dot_tn(a, b):
    return lax.dot_general(a, b, (((0,), (0,)), ((), ())), preferred_element_type=F32)


def _tile(n, pref):
    t = min(n, pref)
    assert n % t == 0, (n, t)
    return t


def _ada_kernel(c_ref, w_ref, b_ref, o_ref):
    ca = _silu(c_ref[...]).astype(BF16)
    o_ref[...] = _dot(ca, w_ref[...].astype(BF16)) + b_ref[...]


def _ada(c, w_ada, b_ada):
    L, D, N = w_ada.shape
    B = c.shape[0]
    tn = _tile(N, 1024)
    return pl.pallas_call(
        _ada_kernel,
        grid=(L, N // tn),
        in_specs=[
            pl.BlockSpec((B, D), lambda l, j: (0, 0)),
            pl.BlockSpec((None, D, tn), lambda l, j: (l, 0, j)),
            pl.BlockSpec((None, 1, tn), lambda l, j: (l, 0, j)),
        ],
        out_specs=pl.BlockSpec((None, B, tn), lambda l, j: (l, 0, j)),
        out_shape=jax.ShapeDtypeStruct((L, B, N), F32),
        compiler_params=_params("parallel", "parallel"),
    )(c, w_ada, b_ada.reshape(L, 1, N))


def _ffn_kernel(x_hbm, mod_ref, modn_ref, g_ref, wg_ref, wu_ref, wd_ref, fin_ref, o_ref,
                xbuf, h_cur, h_next, sem, *, sub, tm, rows, n_slices, final):
    i = pl.program_id(0)
    j = pl.program_id(1)
    has_next = i + 1 < pl.num_programs(0)
    shift, scale, gate_row = 3 * sub, 3 * sub + 1, 3 * sub + 2

    def fetch(tile):
        return pltpu.make_async_copy(x_hbm.at[pl.ds(tile * tm, tm), :], xbuf.at[pl.ds(0, tm), :], sem)

    def norm_rows(r0, nrows, mod_src):
        h = _modulated_norm(xbuf[pl.ds(r0, nrows), :], g_ref[...],
                            mod_src[shift:shift + 1, :], mod_src[scale:scale + 1, :])
        h_next[pl.ds(r0, nrows), :] = h.astype(BF16)
        return h

    @pl.when((i == 0) & (j == 0))
    def _():
        first = fetch(0)
        first.start()
        xbuf[pl.ds(tm, rows), :] = jnp.zeros((rows, xbuf.shape[1]), F32)
        first.wait()
        for s in range(n_slices):
            norm_rows(s * rows, rows, mod_ref)

    @pl.when(j == 0)
    def _():
        o_ref[...] = xbuf[pl.ds(0, tm), :]
        h_cur[...] = h_next[pl.ds(0, tm), :]

    @pl.when((j == 2) & has_next)
    def _():
        fetch(i + 1).wait()

    scheduled = (j >= 2) & (j < 2 + n_slices)
    r0 = pl.multiple_of(jnp.where(scheduled, (j - 2) * rows, tm), rows)
    hn = norm_rows(r0, rows, modn_ref)
    d_model = o_ref.shape[1]
    tf = wg_ref.shape[1]
    bits = pltpu.bitcast(hn, jnp.uint32).reshape(rows // 8, 8, d_model)
    folded = bits[0]
    for r in range(1, rows // 8):
        folded = folded | bits[r]
    lanes = folded[:, 0:tf]
    for k in range(1, d_model // tf):
        lanes = lanes | folded[:, k * tf:(k + 1) * tf]
    half_word = jnp.uint32(16)
    zero = pltpu.bitcast(lax.shift_right_logical(lax.shift_right_logical(lanes, half_word), half_word), F32)

    h = h_cur[...]
    gate = _dot(h, wg_ref[...])
    up = (_dot(h, wu_ref[...]).reshape(tm // 8, 8, tf) + zero).reshape(tm, tf)
    a = (_silu(gate) * up).astype(BF16)
    c = 0.5 * mod_ref[gate_row:gate_row + 1, :]
    slab = min(d_model, FFN_DOWN_SLAB)
    for n in range(d_model // slab):
        sl = slice(n * slab, (n + 1) * slab)
        o_ref[:, sl] += c[:, sl] * _dot(a, wd_ref[:, sl])

    @pl.when((j == 0) & has_next)
    def _():
        fetch(i + 1).start()

    if final:
        @pl.when(j == pl.num_programs(1) - 1)
        def _():
            o_ref[...] = _rms(o_ref[...], fin_ref[...])


def _ffn(x, mod_l, g, wg, wu, wd, fin_g, layer, which, sub, seq, final):
    T, D = x.shape
    F = wg.shape[-1]
    tm = _tile(seq, FFN_TM)
    tf = _tile(F, 512)
    nt, nf = T // tm, F // tf
    per_seq = seq // tm
    n_slices = 1
    while 2 * n_slices <= min(nf - 2, 8):
        n_slices *= 2
    assert nf >= 2 + n_slices and tm % (16 * n_slices) == 0, (nf, tm)
    rows = tm // n_slices
    mod_spec = lambda tile_of: pl.BlockSpec((None, 3 * N_SUB, D), lambda i, j: (tile_of(i) // per_seq, 0, 0))
    return pl.pallas_call(
        functools.partial(_ffn_kernel, sub=sub, tm=tm, rows=rows, n_slices=n_slices, final=final),
        grid=(nt, nf),
        in_specs=[
            pl.BlockSpec(memory_space=pl.ANY),
            mod_spec(lambda i: i),
            mod_spec(lambda i: jnp.minimum(i + 1, nt - 1)),
            pl.BlockSpec((1, D), lambda i, j: (0, 0)),
            pl.BlockSpec((None, None, D, tf), lambda i, j: (layer, which, 0, j)),
            pl.BlockSpec((None, None, D, tf), lambda i, j: (layer, which, 0, j)),
            pl.BlockSpec((None, None, tf, D), lambda i, j: (layer, which, j, 0)),
            pl.BlockSpec((1, D), lambda i, j: (0, 0)),
        ],
        out_specs=pl.BlockSpec((tm, D), lambda i, j: (i, 0)),
        out_shape=jax.ShapeDtypeStruct((T, D), F32),
        scratch_shapes=[pltpu.VMEM((tm + rows, D), F32), pltpu.VMEM((tm, D), BF16),
                        pltpu.VMEM((tm + rows, D), BF16), pltpu.SemaphoreType.DMA(())],
        compiler_params=_params("arbitrary", "arbitrary"),
    )(x, mod_l, mod_l, g, wg, wu, wd, fin_g)


def _inproj_kernel(x_ref, mod_ref, g_ref, w_ref, tab_ref, qg_ref, kg_ref, wq_ref, wk_ref, wv_ref,
                   h_ref, qf_ref, kn_ref, kpe_ref, v_ref, *, sub, tm):
    h = _modulated_norm(x_ref[...], g_ref[...], mod_ref[3 * sub:3 * sub + 1, :],
                        mod_ref[3 * sub + 1:3 * sub + 2, :]).astype(BF16)
    h_ref[...] = h
    proj = _dot(h, w_ref[...])
    qa = proj[:, :MLA_Q_RANK]
    kva = proj[:, MLA_Q_RANK:MLA_Q_RANK + MLA_KV_RANK]
    pe = proj[:, MLA_Q_RANK + MLA_KV_RANK:MLA_Q_RANK + MLA_KV_RANK + V7X_LANES]

    scale = (MLA_NOPE + MLA_ROPE) ** -0.5
    tab = tab_ref[...]
    lane = lax.broadcasted_iota(jnp.int32, (tm, V7X_LANES), 1)
    low = lane < MLA_ROPE

    def rope(y2):
        y = y2 * tab
        return jnp.where(low, y + pltpu.roll(y, MLA_ROPE, axis=1), 0.0)

    cq = _rms(qa, qg_ref[...]).astype(BF16)
    qf = _dot(cq, wq_ref[...])
    ckv = _rms(kva, kg_ref[...]).astype(BF16)
    kn_ref[...] = _dot(ckv, wk_ref[...]).astype(BF16)
    v_ref[...] = _dot(ckv, wv_ref[...]).astype(BF16)
    half = MLA_ROPE // 2
    pe2 = jnp.where(low, pe, jnp.where(lane < MLA_ROPE + half, -pltpu.roll(pe, half, axis=1),
                                       pltpu.roll(pe, V7X_LANES - half, axis=1)))
    kpe_ref[...] = rope(pe2).astype(BF16)
    P = MLA_QK_PAD
    for hd in range(MLA_HEADS):
        qf_ref[:, hd * P:hd * P + MLA_NOPE] = (qf[:, hd * P:hd * P + MLA_NOPE] * scale).astype(BF16)
        qf_ref[:, hd * P + MLA_NOPE:(hd + 1) * P] = (
            rope(qf[:, hd * P + MLA_NOPE:(hd + 1) * P]) * scale).astype(BF16)


def _inproj(x, mod_l, g, w, tab, qg, kg, wq, wk, wv, layer, seq):
    T, D = x.shape
    n_hg = 4 * HG_HEADS * HG_DIM
    N = w.shape[-1] - n_hg
    assert n_hg % N == 0
    col_block = n_hg // N
    tm = _tile(seq, 512)
    per_seq = seq // tm
    H = MLA_HEADS
    row = lambda width: pl.BlockSpec((tm, width), lambda i: (i, 0))
    full = lambda a: pl.BlockSpec(a.shape, lambda i: (0, 0))
    widths = (D, H * MLA_QK_PAD, H * MLA_NOPE, V7X_LANES, H * MLA_V)
    return pl.pallas_call(
        functools.partial(_inproj_kernel, sub=1, tm=tm),
        grid=(T // tm,),
        in_specs=[
            row(D),
            pl.BlockSpec((None, 3 * N_SUB, D), lambda i: (i // per_seq, 0, 0)),
            pl.BlockSpec((1, D), lambda i: (0, 0)),
            pl.BlockSpec((None, D, N), lambda i: (layer, 0, col_block)),
            row(V7X_LANES), full(qg), full(kg), full(wq), full(wk), full(wv),
        ],
        out_specs=[row(wd) for wd in widths],
        out_shape=[jax.ShapeDtypeStruct((T, wd), BF16) for wd in widths],
        compiler_params=_params("parallel"),
    )(x, mod_l, g, w, tab, qg, kg, wq, wk, wv)


def _hgrn_head(hq, z, hi, hg, lb, gn, st, *, ts):
    C = HG_CHUNK
    nc = ts // C
    u = jnp.exp(-jnp.abs(z))
    r = 1.0 / (1.0 + u)
    ur = u * r
    z_pos = z >= 0.0
    log_f = jnp.log(lb + (1.0 - lb) * jnp.where(z_pos, r, ur))
    k = (1.0 - lb) * jnp.where(z_pos, ur, r)
    q = _silu(hq)

    rows8 = 8
    per_chunk = C // rows8
    x = log_f.reshape(ts // rows8, rows8, HG_DIM)
    sub = lax.broadcasted_iota(jnp.int32, (1, rows8, HG_DIM), 1)
    step = 1
    while step < rows8:
        x = x + jnp.where(sub >= step, pltpu.roll(x, step, axis=1), 0.0)
        step *= 2
    x = x.reshape(nc, per_chunk, rows8, HG_DIM)
    parts = [x[:, 0]]
    carry = x[:, 0, rows8 - 1:rows8, :]
    for i in range(1, per_chunk):
        parts.append(x[:, i] + carry)
        if i + 1 < per_chunk:
            carry = carry + x[:, i, rows8 - 1:rows8, :]
    b3 = jnp.concatenate(parts, axis=1)

    b_end = b3[:, C - 1:C, :]
    q3 = q.reshape(nc, C, HG_DIM)
    k3 = k.reshape(nc, C, HG_DIM)
    decay = jnp.exp(b_end)
    q_in = (q3 * jnp.exp(b3)).astype(BF16).reshape(ts, HG_DIM)
    k_back = k3 * jnp.exp(-b3)
    k_in = k_back.astype(BF16).reshape(ts, HG_DIM)
    k_out = (k_back * decay).astype(BF16).reshape(ts, HG_DIM)
    v = hi.astype(BF16)

    G = HG_GROUP
    R = G * C
    ri = lax.broadcasted_iota(jnp.int32, (R, R), 0)
    ci = lax.broadcasted_iota(jnp.int32, (R, R), 1)
    causal = (ci <= ri) & ((ri // C) == (ci // C))
    zeros = jnp.zeros((C, HG_DIM), BF16)

    def block_arranged(m):
        return jnp.concatenate(
            [jnp.concatenate([m[c * C:(c + 1) * C] if cc == c else zeros for cc in range(G)], axis=1)
             for c in range(G)], axis=0)

    outs = []
    for g in range(ts // R):
        sl = slice(g * R, (g + 1) * R)
        a = jnp.where(causal, _dot_nt(q_in[sl], k_in[sl]), 0.0)
        o = _dot(a.astype(BF16), v[sl])
        kv = _dot_tn(v[sl], block_arranged(k_out[sl]))
        starts = []
        for c in range(G):
            starts.append(st)
            st = st * decay[g * G + c] + kv[:, c * HG_DIM:(c + 1) * HG_DIM]
        st_all = jnp.concatenate(starts, axis=1).astype(BF16)
        outs.append(o + _dot_nt(block_arranged(q_in[sl]), st_all))
    o = jnp.concatenate(outs, axis=0)
    return (_rms(o, gn) * _silu(hg)).astype(BF16), st


def _hgrn_kernel(h_ref, wq_ref, wf_ref, wi_ref, wg_ref, lb_ref, gn_ref, o_ref, st_ref, *, ts, heads):
    @pl.when(pl.program_id(2) == 0)
    def _():
        st_ref[...] = jnp.zeros_like(st_ref)

    rb = min(ts, HG_ROW_BLOCK)
    states = [st_ref[h] for h in range(heads)]
    for r in range(ts // rb):
        rows = slice(r * rb, (r + 1) * rb)
        hb = h_ref[rows, :]
        ys = [_dot(hb, w[...]) for w in (wq_ref, wf_ref, wi_ref, wg_ref)]
        for h in range(heads):
            sl = slice(h * HG_DIM, (h + 1) * HG_DIM)
            hq, z, hi, hg = (y[:, sl] for y in ys)
            o, states[h] = _hgrn_head(hq, z, hi, hg, lb_ref[:, sl], gn_ref[...], states[h], ts=rb)
            o_ref[rows, sl] = o
    for h in range(heads):
        st_ref[h] = states[h]


def _hgrn(h, w, lb, gn, layer, batch, seq):
    T, D = h.shape
    ts = _tile(seq, 2048)
    ns = seq // ts
    hp = HG_HEADS_PER_STEP
    nh = HG_HEADS // hp
    wo = hp * HG_DIM

    def wcol(group):
        return pl.BlockSpec((None, D, wo), lambda b, p, s: (layer, 0, group * nh + p))

    return pl.pallas_call(
        functools.partial(_hgrn_kernel, ts=ts, heads=hp),
        grid=(batch, nh, ns),
        in_specs=[pl.BlockSpec((ts, D), lambda b, p, s: (b * ns + s, 0)),
                  wcol(0), wcol(1), wcol(2), wcol(3),
                  pl.BlockSpec((1, wo), lambda b, p, s: (0, p)),
                  pl.BlockSpec((1, HG_DIM), lambda b, p, s: (0, 0))],
        out_specs=pl.BlockSpec((ts, wo), lambda b, p, s: (b * ns + s, p)),
        out_shape=jax.ShapeDtypeStruct((T, HG_HEADS * HG_DIM), BF16),
        scratch_shapes=[pltpu.VMEM((hp, HG_DIM, HG_DIM), F32)],
        compiler_params=_params("parallel", "arbitrary", "arbitrary"),
    )(h, w, w, w, w, lb, gn)


def _attn_kernel(q_ref, kn_ref, kpe_ref, v_ref, o_ref, *, seq, t):
    n = seq // t
    k_all = jnp.concatenate([kn_ref[...], kpe_ref[...]], axis=1)
    keep = (lax.broadcasted_iota(jnp.int32, (t, t), 1)
            <= lax.broadcasted_iota(jnp.int32, (t, t), 0))

    def update(state, s, v):
        row_max = jnp.max(s, axis=-1, keepdims=True)
        if state is None:
            p = jnp.exp(s - row_max)
            return row_max, jnp.sum(p, axis=-1, keepdims=True), _dot(p.astype(BF16), v)
        m, l, acc = state
        m_new = jnp.maximum(m, row_max)
        alpha = jnp.exp(m - m_new)
        p = jnp.exp(s - m_new)
        return (m_new, alpha * l + jnp.sum(p, axis=-1, keepdims=True),
                alpha * acc + _dot(p.astype(BF16), v))

    for i in range(n):
        q = q_ref[i * t:(i + 1) * t, :]
        state = None
        for j in range(i + 1):
            s = _dot_nt(q, k_all[j * t:(j + 1) * t, :])
            if j == i:
                s = jnp.where(keep, s, -jnp.inf)
            state = update(state, s, v_ref[j * t:(j + 1) * t, :])
        _, l, acc = state
        o_ref[i * t:(i + 1) * t, :] = (acc / l).astype(BF16)


def _attn(qf, kn, kpe, v, batch, seq):
    T = qf.shape[0]
    t = _tile(seq, 512)
    H = MLA_HEADS
    return pl.pallas_call(
        functools.partial(_attn_kernel, seq=seq, t=t),
        grid=(batch, H),
        in_specs=[
            pl.BlockSpec((seq, MLA_QK_PAD), lambda b, h: (b, h)),
            pl.BlockSpec((seq, MLA_NOPE), lambda b, h: (b, h)),
            pl.BlockSpec((seq, V7X_LANES), lambda b, h: (b, 0)),
            pl.BlockSpec((seq, MLA_V), lambda b, h: (b, h)),
        ],
        out_specs=pl.BlockSpec((seq, MLA_V), lambda b, h: (b, h)),
        out_shape=jax.ShapeDtypeStruct((T, H * MLA_V), BF16),
        compiler_params=_params("parallel", "parallel"),
    )(qf, kn, kpe, v)


def _outproj_kernel(x_ref, mod_ref, a_ref, b_ref, wa_ref, wb_ref, o_ref, *, sub):
    y = _dot(a_ref[...], wa_ref[...]) + _dot(b_ref[...], wb_ref[...])
    o_ref[...] = x_ref[...] + mod_ref[3 * sub + 2:3 * sub + 3, :] * y


def _outproj(x, mod_l, a, b, wa, wb, seq):
    T, D = x.shape
    tm = _tile(seq, 512)
    per_seq = seq // tm
    return pl.pallas_call(
        functools.partial(_outproj_kernel, sub=1),
        grid=(T // tm,),
        in_specs=[
            pl.BlockSpec((tm, D), lambda i: (i, 0)),
            pl.BlockSpec((None, 3 * N_SUB, D), lambda i: (i // per_seq, 0, 0)),
            pl.BlockSpec((tm, a.shape[1]), lambda i: (i, 0)),
            pl.BlockSpec((tm, b.shape[1]), lambda i: (i, 0)),
            pl.BlockSpec(wa.shape, lambda i: (0, 0)),
            pl.BlockSpec(wb.shape, lambda i: (0, 0)),
        ],
        out_specs=pl.BlockSpec((tm, D), lambda i: (i, 0)),
        out_shape=jax.ShapeDtypeStruct((T, D), F32),
        compiler_params=_params("parallel"),
    )(x, mod_l, a, b, wa, wb)


def _rotate_half_cols(w):
    half = MLA_ROPE // 2
    w1, w2 = w[..., :half], w[..., half:]
    return jnp.concatenate([-w2, w1], axis=-1)


def _prep_w_in(w_in):
    pad = (-w_in.shape[-1]) % (4 * V7X_MXU_DIM)
    return jnp.pad(w_in.astype(BF16), ((0, 0), (0, 0), (0, pad)))


def _prep_w_q(w_q_up):
    L, R, _ = w_q_up.shape
    w = w_q_up.reshape(L, R, MLA_HEADS, MLA_NOPE + MLA_ROPE)
    pe = w[..., MLA_NOPE:]
    w = jnp.concatenate([w, _rotate_half_cols(pe)], axis=-1)
    return w.reshape(L, R, MLA_HEADS * MLA_QK_PAD).astype(BF16)


def _prep_w_kv(w_kv_up):
    L, R, _ = w_kv_up.shape
    w = w_kv_up.reshape(L, R, MLA_HEADS, MLA_NOPE + MLA_V)
    wk = w[..., :MLA_NOPE].reshape(L, R, MLA_HEADS * MLA_NOPE)
    wv = w[..., MLA_NOPE:].reshape(L, R, MLA_HEADS * MLA_V)
    return wk.astype(BF16), wv.astype(BF16)


def kernel(x, c, positions, w_ada, b_ada, norm_g, w_in, qa_norm_g, w_q_up, kva_norm_g, w_kv_up,
           hg_lb_logits, hg_norm_g, w_out, ffn_w_gate, ffn_w_up, ffn_w_down, final_norm_g):
    B, S, D = x.shape
    L = w_ada.shape[0]
    T = B * S

    half = MLA_ROPE // 2
    inv_freq = ROPE_THETA ** (-jnp.arange(half, dtype=F32) / half)
    ang = positions.astype(F32).reshape(T, 1) * inv_freq
    cos, sin = jnp.cos(ang), jnp.sin(ang)
    tab = jnp.concatenate([cos, cos, sin, sin], axis=-1)

    lb_all = jnp.cumsum(jax.nn.softmax(hg_lb_logits.astype(F32), axis=0), axis=0)
    lb_all = lb_all - lb_all[0:1]

    w_in_p = _prep_w_in(w_in)
    w_q_p = _prep_w_q(w_q_up)
    w_k_p, w_v_p = _prep_w_kv(w_kv_up)
    hg_w = HG_HEADS * HG_DIM
    w_out_a = w_out[:, :hg_w, :].astype(BF16)
    w_out_b = w_out[:, hg_w:, :].astype(BF16)
    wg, wu, wd = ffn_w_gate.astype(BF16), ffn_w_up.astype(BF16), ffn_w_down.astype(BF16)

    mod = _ada(c, w_ada, b_ada).reshape(L, B, 3 * N_SUB, D)

    fin_g = final_norm_g.reshape(1, D)
    xf = x.reshape(T, D)
    for l in range(L):
        mod_l = mod[l]
        xf = _ffn(xf, mod_l, norm_g[l, 0:1], wg, wu, wd, fin_g, l, 0, 0, S, False)
        h, qf, kn, kpe, v = _inproj(xf, mod_l, norm_g[l, 1:2], w_in_p, tab, qa_norm_g[l:l + 1],
                                    kva_norm_g[l:l + 1], w_q_p[l], w_k_p[l], w_v_p[l], l, S)
        o_hg = _hgrn(h, w_in_p, lb_all[l:l + 1], hg_norm_g[l:l + 1], l, B, S)
        o_mla = _attn(qf, kn, kpe, v, B, S)
        xf = _outproj(xf, mod_l, o_hg, o_mla, w_out_a[l], w_out_b[l], S)
        xf = _ffn(xf, mod_l, norm_g[l, 2:3], wg, wu, wd, fin_g, l, 1, 2, S, l == L - 1)
    return xf.reshape(B, S, D)
```

```python
import functools

import jax
import jax.numpy as jnp
from jax import lax
from jax.experimental import pallas as pl
from jax.experimental.pallas import tpu as pltpu

EPS = 1e-6
N_SUB = 3
HG_HEADS = 8
HG_DIM = 128
HG_CHUNK = 32
HG_GROUP = 4
HG_HEADS_PER_STEP = 4
HG_ROW_BLOCK = 512
MLA_HEADS = 8
MLA_NOPE = 128
MLA_ROPE = 64
MLA_V = 128
MLA_Q_RANK = 512
MLA_KV_RANK = 256
ROPE_THETA = 10000.0
MLA_QK_PAD = 256
FFN_DOWN_SLAB = 512
FFN_TM = 1024

V7X_LANES = 128
V7X_MXU_DIM = 256
V7X_VMEM_BYTES = 64 * 1024 * 1024
VMEM_LIMIT = V7X_VMEM_BYTES - 3 * 1024 * 1024

F32 = jnp.float32
BF16 = jnp.bfloat16


def _params(*sem):
    return pltpu.CompilerParams(dimension_semantics=sem, vmem_limit_bytes=VMEM_LIMIT)


def _sigmoid(x):
    return 1.0 / (1.0 + jnp.exp(-x))


def _silu(x):
    return x * _sigmoid(x)


def _rms(x, g):
    ms = jnp.mean(x * x, axis=-1, keepdims=True)
    return (x * lax.rsqrt(ms + EPS)) * g


def _modulated_norm(x, g, shift, scale):
    return _rms(x, g) * (1.0 + scale) + shift


def _dot(a, b):
    return jnp.dot(a, b, preferred_element_type=F32)


def _dot_nt(a, b):
    return lax.dot_general(a, b, (((1,), (1,)), ((), ())), preferred_element_type=F32)


def _dot_tn(a, b):
    return lax.dot_general(a, b, (((0,), (0,)), ((), ())), preferred_element_type=F32)


def _tile(n, pref):
    t = min(n, pref)
    assert n % t == 0, (n, t)
    return t


def _ada_kernel(c_ref, w_ref, b_ref, o_ref):
    ca = _silu(c_ref[...]).astype(BF16)
    o_ref[...] = _dot(ca, w_ref[...].astype(BF16)) + b_ref[...]


def _ada(c, w_ada, b_ada):
    L, D, N = w_ada.shape
    B = c.shape[0]
    tn = _tile(N, 1024)
    return pl.pallas_call(
        _ada_kernel,
        grid=(L, N // tn),
        in_specs=[
            pl.BlockSpec((B, D), lambda l, j: (0, 0)),
            pl.BlockSpec((None, D, tn), lambda l, j: (l, 0, j)),
            pl.BlockSpec((None, 1, tn), lambda l, j: (l, 0, j)),
        ],
        out_specs=pl.BlockSpec((None, B, tn), lambda l, j: (l, 0, j)),
        out_shape=jax.ShapeDtypeStruct((L, B, N), F32),
        compiler_params=_params("parallel", "parallel"),
    )(c, w_ada, b_ada.reshape(L, 1, N))


def _ffn_kernel(x_hbm, mod_ref, modn_ref, g_ref, wg_ref, wu_ref, wd_ref, fin_ref, o_ref,
                xbuf, h_cur, h_next, sem, *, sub, tm, rows, n_slices, final):
    i = pl.program_id(0)
    j = pl.program_id(1)
    has_next = i + 1 < pl.num_programs(0)
    shift, scale, gate_row = 3 * sub, 3 * sub + 1, 3 * sub + 2

    def fetch(tile):
        return pltpu.make_async_copy(x_hbm.at[pl.ds(tile * tm, tm), :], xbuf.at[pl.ds(0, tm), :], sem)

    def norm_rows(r0, nrows, mod_src):
        h = _modulated_norm(xbuf[pl.ds(r0, nrows), :], g_ref[...],
                            mod_src[shift:shift + 1, :], mod_src[scale:scale + 1, :])
        h_next[pl.ds(r0, nrows), :] = h.astype(BF16)
        return h

    @pl.when((i == 0) & (j == 0))
    def _():
        first = fetch(0)
        first.start()
        xbuf[pl.ds(tm, rows), :] = jnp.zeros((rows, xbuf.shape[1]), F32)
        first.wait()
        for s in range(n_slices):
            norm_rows(s * rows, rows, mod_ref)

    @pl.when(j == 0)
    def _():
        o_ref[...] = xbuf[pl.ds(0, tm), :]
        h_cur[...] = h_next[pl.ds(0, tm), :]

    @pl.when((j == 2) & has_next)
    def _():
        fetch(i + 1).wait()

    scheduled = (j >= 2) & (j < 2 + n_slices)
    r0 = pl.multiple_of(jnp.where(scheduled, (j - 2) * rows, tm), rows)
    hn = norm_rows(r0, rows, modn_ref)
    d_model = o_ref.shape[1]
    tf = wg_ref.shape[1]
    bits = pltpu.bitcast(hn, jnp.uint32).reshape(rows // 8, 8, d_model)
    folded = bits[0]
    for r in range(1, rows // 8):
        folded = folded | bits[r]
    lanes = folded[:, 0:tf]
    for k in range(1, d_model // tf):
        lanes = lanes | folded[:, k * tf:(k + 1) * tf]
    half_word = jnp.uint32(16)
    zero = pltpu.bitcast(lax.shift_right_logical(lax.shift_right_logical(lanes, half_word), half_word), F32)

    h = h_cur[...]
    gate = (_dot(h, wg_ref[...]).reshape(tm // 8, 8, tf) + zero).reshape(tm, tf)
    up = _dot(h, wu_ref[...])
    a = (_silu(gate) * up).astype(BF16)
    c = 0.5 * mod_ref[gate_row:gate_row + 1, :]
    slab = min(d_model, FFN_DOWN_SLAB)
    for n in range(d_model // slab):
        sl = slice(n * slab, (n + 1) * slab)
        o_ref[:, sl] += c[:, sl] * _dot(a, wd_ref[:, sl])

    @pl.when((j == 0) & has_next)
    def _():
        fetch(i + 1).start()

    if final:
        @pl.when(j == pl.num_programs(1) - 1)
        def _():
            o_ref[...] = _rms(o_ref[...], fin_ref[...])


def _ffn(x, mod_l, g, wg, wu, wd, fin_g, layer, which, sub, seq, final):
    T, D = x.shape
    F = wg.shape[-1]
    tm = _tile(seq, FFN_TM)
    tf = _tile(F, 512)
    nt, nf = T // tm, F // tf
    per_seq = seq // tm
    n_slices = 1
    while 2 * n_slices <= min(nf - 2, 8):
        n_slices *= 2
    assert nf >= 2 + n_slices and tm % (16 * n_slices) == 0, (nf, tm)
    rows = tm // n_slices
    mod_spec = lambda tile_of: pl.BlockSpec((None, 3 * N_SUB, D), lambda i, j: (tile_of(i) // per_seq, 0, 0))
    return pl.pallas_call(
        functools.partial(_ffn_kernel, sub=sub, tm=tm, rows=rows, n_slices=n_slices, final=final),
        grid=(nt, nf),
        in_specs=[
            pl.BlockSpec(memory_space=pl.ANY),
            mod_spec(lambda i: i),
            mod_spec(lambda i: jnp.minimum(i + 1, nt - 1)),
            pl.BlockSpec((1, D), lambda i, j: (0, 0)),
            pl.BlockSpec((None, None, D, tf), lambda i, j: (layer, which, 0, j)),
            pl.BlockSpec((None, None, D, tf), lambda i, j: (layer, which, 0, j)),
            pl.BlockSpec((None, None, tf, D), lambda i, j: (layer, which, j, 0)),
            pl.BlockSpec((1, D), lambda i, j: (0, 0)),
        ],
        out_specs=pl.BlockSpec((tm, D), lambda i, j: (i, 0)),
        out_shape=jax.ShapeDtypeStruct((T, D), F32),
        scratch_shapes=[pltpu.VMEM((tm + rows, D), F32), pltpu.VMEM((tm, D), BF16),
                        pltpu.VMEM((tm + rows, D), BF16), pltpu.SemaphoreType.DMA(())],
        compiler_params=_params("arbitrary", "arbitrary"),
    )(x, mod_l, mod_l, g, wg, wu, wd, fin_g)


def _inproj_kernel(x_ref, mod_ref, g_ref, w_ref, tab_ref, qg_ref, kg_ref, wq_ref, wk_ref, wv_ref,
                   h_ref, qf_ref, kn_ref, kpe_ref, v_ref, *, sub, tm, rb):
    scale = (MLA_NOPE + MLA_ROPE) ** -0.5
    lane = lax.broadcasted_iota(jnp.int32, (rb, V7X_LANES), 1)
    low = lane < MLA_ROPE
    half = MLA_ROPE // 2
    P = MLA_QK_PAD

    for r in range(tm // rb):
        rows = slice(r * rb, (r + 1) * rb)
        tab = tab_ref[rows, :]

        def rope(y2):
            y = y2 * tab
            return jnp.where(low, y + pltpu.roll(y, MLA_ROPE, axis=1), 0.0)

        h = _modulated_norm(x_ref[rows, :], g_ref[...], mod_ref[3 * sub:3 * sub + 1, :],
                            mod_ref[3 * sub + 1:3 * sub + 2, :]).astype(BF16)
        h_ref[rows, :] = h
        proj = _dot(h, w_ref[...])
        qa = proj[:, :MLA_Q_RANK]
        kva = proj[:, MLA_Q_RANK:MLA_Q_RANK + MLA_KV_RANK]
        pe = proj[:, MLA_Q_RANK + MLA_KV_RANK:MLA_Q_RANK + MLA_KV_RANK + V7X_LANES]

        cq = _rms(qa, qg_ref[...]).astype(BF16)
        qf = _dot(cq, wq_ref[...])
        ckv = _rms(kva, kg_ref[...]).astype(BF16)
        kn_ref[rows, :] = _dot(ckv, wk_ref[...]).astype(BF16)
        v_ref[rows, :] = _dot(ckv, wv_ref[...]).astype(BF16)
        pe2 = jnp.where(low, pe, jnp.where(lane < MLA_ROPE + half, -pltpu.roll(pe, half, axis=1),
                                           pltpu.roll(pe, V7X_LANES - half, axis=1)))
        kpe_ref[rows, :] = rope(pe2).astype(BF16)
        for hd in range(MLA_HEADS):
            qf_ref[rows, hd * P:hd * P + MLA_NOPE] = (qf[:, hd * P:hd * P + MLA_NOPE] * scale).astype(BF16)
            qf_ref[rows, hd * P + MLA_NOPE:(hd + 1) * P] = (
                rope(qf[:, hd * P + MLA_NOPE:(hd + 1) * P]) * scale).astype(BF16)


def _inproj(x, mod_l, g, w, tab, qg, kg, wq, wk, wv, layer, seq):
    T, D = x.shape
    n_hg = 4 * HG_HEADS * HG_DIM
    N = w.shape[-1] - n_hg
    assert n_hg % N == 0
    col_block = n_hg // N
    tm = _tile(seq, 1024)
    rb = _tile(tm, 512)
    per_seq = seq // tm
    H = MLA_HEADS
    row = lambda width: pl.BlockSpec((tm, width), lambda i: (i, 0))
    full = lambda a: pl.BlockSpec(a.shape, lambda i: (0, 0), pipeline_mode=pl.Buffered(1))
    widths = (D, H * MLA_QK_PAD, H * MLA_NOPE, V7X_LANES, H * MLA_V)
    return pl.pallas_call(
        functools.partial(_inproj_kernel, sub=1, tm=tm, rb=rb),
        grid=(T // tm,),
        in_specs=[
            row(D),
            pl.BlockSpec((None, 3 * N_SUB, D), lambda i: (i // per_seq, 0, 0)),
            pl.BlockSpec((1, D), lambda i: (0, 0)),
            pl.BlockSpec((None, D, N), lambda i: (layer, 0, col_block), pipeline_mode=pl.Buffered(1)),
            row(V7X_LANES), full(qg), full(kg), full(wq), full(wk), full(wv),
        ],
        out_specs=[row(wd) for wd in widths],
        out_shape=[jax.ShapeDtypeStruct((T, wd), BF16) for wd in widths],
        compiler_params=_params("parallel"),
    )(x, mod_l, g, w, tab, qg, kg, wq, wk, wv)


def _hgrn_head(hq, z, hi, hg, lb, gn, st, *, ts):
    C = HG_CHUNK
    nc = ts // C
    u = jnp.exp(-jnp.abs(z))
    r = 1.0 / (1.0 + u)
    ur = u * r
    z_pos = z >= 0.0
    log_f = jnp.log(lb + (1.0 - lb) * jnp.where(z_pos, r, ur))
    k = (1.0 - lb) * jnp.where(z_pos, ur, r)
    q = _silu(hq)

    rows8 = 8
    per_chunk = C // rows8
    x = log_f.reshape(ts // rows8, rows8, HG_DIM)
    sub = lax.broadcasted_iota(jnp.int32, (1, rows8, HG_DIM), 1)
    step = 1
    while step < rows8:
        x = x + jnp.where(sub >= step, pltpu.roll(x, step, axis=1), 0.0)
        step *= 2
    x = x.reshape(nc, per_chunk, rows8, HG_DIM)
    parts = [x[:, 0]]
    carry = x[:, 0, rows8 - 1:rows8, :]
    for i in range(1, per_chunk):
        parts.append(x[:, i] + carry)
        if i + 1 < per_chunk:
            carry = carry + x[:, i, rows8 - 1:rows8, :]
    b3 = jnp.concatenate(parts, axis=1)

    b_end = b3[:, C - 1:C, :]
    q3 = q.reshape(nc, C, HG_DIM)
    k3 = k.reshape(nc, C, HG_DIM)
    decay = jnp.exp(b_end)
    q_in = (q3 * jnp.exp(b3)).astype(BF16).reshape(ts, HG_DIM)
    k_back = k3 * jnp.exp(-b3)
    k_in = k_back.astype(BF16).reshape(ts, HG_DIM)
    k_out = (k_back * decay).astype(BF16).reshape(ts, HG_DIM)
    v = hi.astype(BF16)

    G = HG_GROUP
    R = G * C
    ri = lax.broadcasted_iota(jnp.int32, (R, R), 0)
    ci = lax.broadcasted_iota(jnp.int32, (R, R), 1)
    causal = (ci <= ri) & ((ri // C) == (ci // C))
    zeros = jnp.zeros((C, HG_DIM), BF16)

    def block_arranged(m):
        return jnp.concatenate(
            [jnp.concatenate([m[c * C:(c + 1) * C] if cc == c else zeros for cc in range(G)], axis=1)
             for c in range(G)], axis=0)

    outs = []
    for g in range(ts // R):
        sl = slice(g * R, (g + 1) * R)
        a = jnp.where(causal, _dot_nt(q_in[sl], k_in[sl]), 0.0)
        o = _dot(a.astype(BF16), v[sl])
        kv = _dot_tn(v[sl], block_arranged(k_out[sl]))
        starts = []
        for c in range(G):
            starts.append(st)
            st = st * decay[g * G + c] + kv[:, c * HG_DIM:(c + 1) * HG_DIM]
        st_all = jnp.concatenate(starts, axis=1).astype(BF16)
        outs.append(o + _dot_nt(block_arranged(q_in[sl]), st_all))
    o = jnp.concatenate(outs, axis=0)
    return (_rms(o, gn) * _silu(hg)).astype(BF16), st


def _hgrn_kernel(h_ref, wq_ref, wf_ref, wi_ref, wg_ref, lb_ref, gn_ref, o_ref, st_ref, *, ts, heads):
    @pl.when(pl.program_id(2) == 0)
    def _():
        st_ref[...] = jnp.zeros_like(st_ref)

    rb = min(ts, HG_ROW_BLOCK)
    states = [st_ref[h] for h in range(heads)]
    for r in range(ts // rb):
        rows = slice(r * rb, (r + 1) * rb)
        hb = h_ref[rows, :]
        ys = [_dot(hb, w[...]) for w in (wq_ref, wf_ref, wi_ref, wg_ref)]
        for h in range(heads):
            sl = slice(h * HG_DIM, (h + 1) * HG_DIM)
            hq, z, hi, hg = (y[:, sl] for y in ys)
            o, states[h] = _hgrn_head(hq, z, hi, hg, lb_ref[:, sl], gn_ref[...], states[h], ts=rb)
            o_ref[rows, sl] = o
    for h in range(heads):
        st_ref[h] = states[h]


def _hgrn(h, w, lb, gn, layer, batch, seq):
    T, D = h.shape
    ts = _tile(seq, 2048)
    ns = seq // ts
    hp = HG_HEADS_PER_STEP
    nh = HG_HEADS // hp
    wo = hp * HG_DIM

    def wcol(group):
        return pl.BlockSpec((None, D, wo), lambda b, p, s: (layer, 0, group * nh + p))

    return pl.pallas_call(
        functools.partial(_hgrn_kernel, ts=ts, heads=hp),
        grid=(batch, nh, ns),
        in_specs=[pl.BlockSpec((ts, D), lambda b, p, s: (b * ns + s, 0)),
                  wcol(0), wcol(1), wcol(2), wcol(3),
                  pl.BlockSpec((1, wo), lambda b, p, s: (0, p)),
                  pl.BlockSpec((1, HG_DIM), lambda b, p, s: (0, 0))],
        out_specs=pl.BlockSpec((ts, wo), lambda b, p, s: (b * ns + s, p)),
        out_shape=jax.ShapeDtypeStruct((T, HG_HEADS * HG_DIM), BF16),
        scratch_shapes=[pltpu.VMEM((hp, HG_DIM, HG_DIM), F32)],
        compiler_params=_params("parallel", "arbitrary", "arbitrary"),
    )(h, w, w, w, w, lb, gn)


def _attn_kernel(q_ref, kn_ref, kpe_ref, v_ref, o_ref, *, seq, t):
    n = seq // t
    k_all = jnp.concatenate([kn_ref[...], kpe_ref[...]], axis=1)
    keep = (lax.broadcasted_iota(jnp.int32, (t, t), 1)
            <= lax.broadcasted_iota(jnp.int32, (t, t), 0))

    def update(state, s, v):
        row_max = jnp.max(s, axis=-1, keepdims=True)
        if state is None:
            p = jnp.exp(s - row_max)
            return row_max, jnp.sum(p, axis=-1, keepdims=True), _dot(p.astype(BF16), v)
        m, l, acc = state
        m_new = jnp.maximum(m, row_max)
        alpha = jnp.exp(m - m_new)
        p = jnp.exp(s - m_new)
        return (m_new, alpha * l + jnp.sum(p, axis=-1, keepdims=True),
                alpha * acc + _dot(p.astype(BF16), v))

    for i in range(n):
        q = q_ref[i * t:(i + 1) * t, :]
        state = None
        for j in range(i + 1):
            s = _dot_nt(q, k_all[j * t:(j + 1) * t, :])
            if j == i:
                s = jnp.where(keep, s, -jnp.inf)
            state = update(state, s, v_ref[j * t:(j + 1) * t, :])
        _, l, acc = state
        o_ref[i * t:(i + 1) * t, :] = (acc / l).astype(BF16)


def _attn(qf, kn, kpe, v, batch, seq):
    T = qf.shape[0]
    t = _tile(seq, 512)
    H = MLA_HEADS
    return pl.pallas_call(
        functools.partial(_attn_kernel, seq=seq, t=t),
        grid=(batch, H),
        in_specs=[
            pl.BlockSpec((seq, MLA_QK_PAD), lambda b, h: (b, h)),
            pl.BlockSpec((seq, MLA_NOPE), lambda b, h: (b, h)),
            pl.BlockSpec((seq, V7X_LANES), lambda b, h: (b, 0)),
            pl.BlockSpec((seq, MLA_V), lambda b, h: (b, h)),
        ],
        out_specs=pl.BlockSpec((seq, MLA_V), lambda b, h: (b, h)),
        out_shape=jax.ShapeDtypeStruct((T, H * MLA_V), BF16),
        compiler_params=_params("parallel", "parallel"),
    )(qf, kn, kpe, v)


def _outproj_kernel(x_ref, mod_ref, a_ref, b_ref, wa_ref, wb_ref, o_ref, *, sub):
    y = _dot(a_ref[...], wa_ref[...]) + _dot(b_ref[...], wb_ref[...])
    o_ref[...] = x_ref[...] + mod_ref[3 * sub + 2:3 * sub + 3, :] * y


def _outproj(x, mod_l, a, b, wa, wb, seq):
    T, D = x.shape
    tm = _tile(seq, 512)
    per_seq = seq // tm
    return pl.pallas_call(
        functools.partial(_outproj_kernel, sub=1),
        grid=(T // tm,),
        in_specs=[
            pl.BlockSpec((tm, D), lambda i: (i, 0)),
            pl.BlockSpec((None, 3 * N_SUB, D), lambda i: (i // per_seq, 0, 0)),
            pl.BlockSpec((tm, a.shape[1]), lambda i: (i, 0)),
            pl.BlockSpec((tm, b.shape[1]), lambda i: (i, 0)),
            pl.BlockSpec(wa.shape, lambda i: (0, 0)),
            pl.BlockSpec(wb.shape, lambda i: (0, 0)),
        ],
        out_specs=pl.BlockSpec((tm, D), lambda i: (i, 0)),
        out_shape=jax.ShapeDtypeStruct((T, D), F32),
        compiler_params=_params("parallel"),
    )(x, mod_l, a, b, wa, wb)


def _rotate_half_cols(w):
    half = MLA_ROPE // 2
    w1, w2 = w[..., :half], w[..., half:]
    return jnp.concatenate([-w2, w1], axis=-1)


def _prep_w_in(w_in):
    pad = (-w_in.shape[-1]) % (4 * V7X_MXU_DIM)
    return jnp.pad(w_in.astype(BF16), ((0, 0), (0, 0), (0, pad)))


def _prep_w_q(w_q_up):
    L, R, _ = w_q_up.shape
    w = w_q_up.reshape(L, R, MLA_HEADS, MLA_NOPE + MLA_ROPE)
    pe = w[..., MLA_NOPE:]
    w = jnp.concatenate([w, _rotate_half_cols(pe)], axis=-1)
    return w.reshape(L, R, MLA_HEADS * MLA_QK_PAD).astype(BF16)


def _prep_w_kv(w_kv_up):
    L, R, _ = w_kv_up.shape
    w = w_kv_up.reshape(L, R, MLA_HEADS, MLA_NOPE + MLA_V)
    wk = w[..., :MLA_NOPE].reshape(L, R, MLA_HEADS * MLA_NOPE)
    wv = w[..., MLA_NOPE:].reshape(L, R, MLA_HEADS * MLA_V)
    return wk.astype(BF16), wv.astype(BF16)


def kernel(x, c, positions, w_ada, b_ada, norm_g, w_in, qa_norm_g, w_q_up, kva_norm_g, w_kv_up,
           hg_lb_logits, hg_norm_g, w_out, ffn_w_gate, ffn_w_up, ffn_w_down, final_norm_g):
    B, S, D = x.shape
    L = w_ada.shape[0]
    T = B * S

    half = MLA_ROPE // 2
    inv_freq = ROPE_THETA ** (-jnp.arange(half, dtype=F32) / half)
    ang = positions.astype(F32).reshape(T, 1) * inv_freq
    cos, sin = jnp.cos(ang), jnp.sin(ang)
    tab = jnp.concatenate([cos, cos, sin, sin], axis=-1)

    lb_all = jnp.cumsum(jax.nn.softmax(hg_lb_logits.astype(F32), axis=0), axis=0)
    lb_all = lb_all - lb_all[0:1]

    w_in_p = _prep_w_in(w_in)
    w_q_p = _prep_w_q(w_q_up)
    w_k_p, w_v_p = _prep_w_kv(w_kv_up)
    hg_w = HG_HEADS * HG_DIM
    w_out_a = w_out[:, :hg_w, :].astype(BF16)
    w_out_b = w_out[:, hg_w:, :].astype(BF16)
    wg, wu, wd = ffn_w_gate.astype(BF16), ffn_w_up.astype(BF16), ffn_w_down.astype(BF16)

    mod = _ada(c, w_ada, b_ada).reshape(L, B, 3 * N_SUB, D)

    fin_g = final_norm_g.reshape(1, D)
    xf = x.reshape(T, D)
    for l in range(L):
        mod_l = mod[l]
        xf = _ffn(xf, mod_l, norm_g[l, 0:1], wg, wu, wd, fin_g, l, 0, 0, S, False)
        h, qf, kn, kpe, v = _inproj(xf, mod_l, norm_g[l, 1:2], w_in_p, tab, qa_norm_g[l:l + 1],
                                    kva_norm_g[l:l + 1], w_q_p[l], w_k_p[l], w_v_p[l], l, S)
        o_hg = _hgrn(h, w_in_p, lb_all[l:l + 1], hg_norm_g[l:l + 1], l, B, S)
        o_mla = _attn(qf, kn, kpe, v, B, S)
        xf = _outproj(xf, mod_l, o_hg, o_mla, w_out_a[l], w_out_b[l], S)
        xf = _ffn(xf, mod_l, norm_g[l, 2:3], wg, wu, wd, fin_g, l, 1, 2, S, l == L - 1)
    return xf.reshape(B, S, D)
```

```python
import functools

import jax
import jax.numpy as jnp
from jax import lax
from jax.experimental import pallas as pl
from jax.experimental.pallas import tpu as pltpu

EPS = 1e-6
N_SUB = 3
HG_HEADS = 8
HG_DIM = 128
HG_CHUNK = 32
HG_GROUP = 4
HG_HEADS_PER_STEP = 4
HG_ROW_BLOCK = 512
MLA_HEADS = 8
MLA_NOPE = 128
MLA_ROPE = 64
MLA_V = 128
MLA_Q_RANK = 512
MLA_KV_RANK = 256
ROPE_THETA = 10000.0
MLA_QK_PAD = 256
FFN_DOWN_SLAB = 512
FFN_TM = 1024

V7X_LANES = 128
V7X_MXU_DIM = 256
V7X_VMEM_BYTES = 64 * 1024 * 1024
VMEM_LIMIT = V7X_VMEM_BYTES - 3 * 1024 * 1024

F32 = jnp.float32
BF16 = jnp.bfloat16


def _params(*sem):
    return pltpu.CompilerParams(dimension_semantics=sem, vmem_limit_bytes=VMEM_LIMIT)


def _sigmoid(x):
    return 1.0 / (1.0 + jnp.exp(-x))


def _silu(x):
    return x * _sigmoid(x)


def _rms(x, g):
    ms = jnp.mean(x * x, axis=-1, keepdims=True)
    return (x * lax.rsqrt(ms + EPS)) * g


def _modulated_norm(x, g, shift, scale):
    return _rms(x, g) * (1.0 + scale) + shift


def _dot(a, b):
    return jnp.dot(a, b, preferred_element_type=F32)


def _dot_nt(a, b):
    return lax.dot_general(a, b, (((1,), (1,)), ((), ())), preferred_element_type=F32)


def _dot_tn(a, b):
    return lax.dot_general(a, b, (((0,), (0,)), ((), ())), preferred_element_type=F32)


def _tile(n, pref):
    t = min(n, pref)
    assert n % t == 0, (n, t)
    return t


def _ada_kernel(c_ref, w_ref, b_ref, o_ref):
    ca = _silu(c_ref[...]).astype(BF16)
    o_ref[...] = _dot(ca, w_ref[...].astype(BF16)) + b_ref[...]


def _ada(c, w_ada, b_ada):
    L, D, N = w_ada.shape
    B = c.shape[0]
    tn = _tile(N, 1024)
    return pl.pallas_call(
        _ada_kernel,
        grid=(L, N // tn),
        in_specs=[
            pl.BlockSpec((B, D), lambda l, j: (0, 0)),
            pl.BlockSpec((None, D, tn), lambda l, j: (l, 0, j)),
            pl.BlockSpec((None, 1, tn), lambda l, j: (l, 0, j)),
        ],
        out_specs=pl.BlockSpec((None, B, tn), lambda l, j: (l, 0, j)),
        out_shape=jax.ShapeDtypeStruct((L, B, N), F32),
        compiler_params=_params("parallel", "parallel"),
    )(c, w_ada, b_ada.reshape(L, 1, N))


def _ffn_kernel(x_hbm, mod_ref, modn_ref, g_ref, wg_ref, wu_ref, wd_ref, fin_ref, o_ref,
                xbuf, h_cur, h_next, sem, *, sub, tm, rows, n_slices, final):
    i = pl.program_id(0)
    j = pl.program_id(1)
    has_next = i + 1 < pl.num_programs(0)
    shift, scale, gate_row = 3 * sub, 3 * sub + 1, 3 * sub + 2

    def fetch(tile):
        return pltpu.make_async_copy(x_hbm.at[pl.ds(tile * tm, tm), :], xbuf.at[pl.ds(0, tm), :], sem)

    def norm_rows(r0, nrows, mod_src):
        h = _modulated_norm(xbuf[pl.ds(r0, nrows), :], g_ref[...],
                            mod_src[shift:shift + 1, :], mod_src[scale:scale + 1, :])
        h_next[pl.ds(r0, nrows), :] = h.astype(BF16)
        return h

    @pl.when((i == 0) & (j == 0))
    def _():
        first = fetch(0)
        first.start()
        xbuf[pl.ds(tm, rows), :] = jnp.zeros((rows, xbuf.shape[1]), F32)
        first.wait()
        for s in range(n_slices):
            norm_rows(s * rows, rows, mod_ref)

    @pl.when(j == 0)
    def _():
        o_ref[...] = xbuf[pl.ds(0, tm), :]
        h_cur[...] = h_next[pl.ds(0, tm), :]

    @pl.when((j == 2) & has_next)
    def _():
        fetch(i + 1).wait()

    scheduled = (j >= 2) & (j < 2 + n_slices)
    r0 = pl.multiple_of(jnp.where(scheduled, (j - 2) * rows, tm), rows)
    hn = norm_rows(r0, rows, modn_ref)
    d_model = o_ref.shape[1]
    tf = wg_ref.shape[1]
    bits = pltpu.bitcast(hn, jnp.uint32).reshape(rows // 8, 8, d_model)
    folded = bits[0]
    for r in range(1, rows // 8):
        folded = folded | bits[r]
    lanes = folded[:, 0:tf]
    for k in range(1, d_model // tf):
        lanes = lanes | folded[:, k * tf:(k + 1) * tf]
    half_word = jnp.uint32(16)
    zero = pltpu.bitcast(lax.shift_right_logical(lax.shift_right_logical(lanes, half_word), half_word), F32)

    h = h_cur[...]
    gate = _dot(h, wg_ref[...])
    up = (_dot(h, wu_ref[...]).reshape(tm // 8, 8, tf) + zero).reshape(tm, tf)
    a = (_silu(gate) * up).astype(BF16)
    c = 0.5 * mod_ref[gate_row:gate_row + 1, :]
    slab = min(d_model, FFN_DOWN_SLAB)
    for n in range(d_model // slab):
        sl = slice(n * slab, (n + 1) * slab)
        o_ref[:, sl] += c[:, sl] * _dot(a, wd_ref[:, sl].astype(BF16))

    @pl.when((j == 0) & has_next)
    def _():
        fetch(i + 1).start()

    if final:
        @pl.when(j == pl.num_programs(1) - 1)
        def _():
            o_ref[...] = _rms(o_ref[...], fin_ref[...])


def _ffn(x, mod_l, g, wg, wu, wd, fin_g, layer, which, sub, seq, final):
    T, D = x.shape
    F = wg.shape[-1]
    tm = _tile(seq, FFN_TM)
    tf = _tile(F, 512)
    nt, nf = T // tm, F // tf
    per_seq = seq // tm
    n_slices = 1
    while 2 * n_slices <= min(nf - 2, 8):
        n_slices *= 2
    assert nf >= 2 + n_slices and tm % (16 * n_slices) == 0, (nf, tm)
    rows = tm // n_slices
    mod_spec = lambda tile_of: pl.BlockSpec((None, 3 * N_SUB, D), lambda i, j: (tile_of(i) // per_seq, 0, 0))
    return pl.pallas_call(
        functools.partial(_ffn_kernel, sub=sub, tm=tm, rows=rows, n_slices=n_slices, final=final),
        grid=(nt, nf),
        in_specs=[
            pl.BlockSpec(memory_space=pl.ANY),
            mod_spec(lambda i: i),
            mod_spec(lambda i: jnp.minimum(i + 1, nt - 1)),
            pl.BlockSpec((1, D), lambda i, j: (0, 0)),
            pl.BlockSpec((None, None, D, tf), lambda i, j: (layer, which, 0, j)),
            pl.BlockSpec((None, None, D, tf), lambda i, j: (layer, which, 0, j)),
            pl.BlockSpec((None, None, tf, D), lambda i, j: (layer, which, j, 0)),
            pl.BlockSpec((1, D), lambda i, j: (0, 0)),
        ],
        out_specs=pl.BlockSpec((tm, D), lambda i, j: (i, 0)),
        out_shape=jax.ShapeDtypeStruct((T, D), F32),
        scratch_shapes=[pltpu.VMEM((tm + rows, D), F32), pltpu.VMEM((tm, D), BF16),
                        pltpu.VMEM((tm + rows, D), BF16), pltpu.SemaphoreType.DMA(())],
        compiler_params=_params("arbitrary", "arbitrary"),
    )(x, mod_l, mod_l, g, wg, wu, wd, fin_g)


def _inproj_kernel(x_ref, mod_ref, g_ref, w_ref, tab_ref, qg_ref, kg_ref, wq_ref, wk_ref, wv_ref,
                   h_ref, qf_ref, kn_ref, kpe_ref, v_ref, *, sub, tm, rb):
    scale = (MLA_NOPE + MLA_ROPE) ** -0.5
    lane = lax.broadcasted_iota(jnp.int32, (rb, V7X_LANES), 1)
    low = lane < MLA_ROPE
    half = MLA_ROPE // 2
    P = MLA_QK_PAD

    for r in range(tm // rb):
        rows = slice(r * rb, (r + 1) * rb)
        tab = tab_ref[rows, :]

        def rope(y2):
            y = y2 * tab
            return jnp.where(low, y + pltpu.roll(y, MLA_ROPE, axis=1), 0.0)

        h = _modulated_norm(x_ref[rows, :], g_ref[...], mod_ref[3 * sub:3 * sub + 1, :],
                            mod_ref[3 * sub + 1:3 * sub + 2, :]).astype(BF16)
        h_ref[rows, :] = h
        proj = _dot(h, w_ref[...])
        qa = proj[:, :MLA_Q_RANK]
        kva = proj[:, MLA_Q_RANK:MLA_Q_RANK + MLA_KV_RANK]
        pe = proj[:, MLA_Q_RANK + MLA_KV_RANK:MLA_Q_RANK + MLA_KV_RANK + V7X_LANES]

        cq = _rms(qa, qg_ref[...]).astype(BF16)
        qf = _dot(cq, wq_ref[...])
        ckv = _rms(kva, kg_ref[...]).astype(BF16)
        kn_ref[rows, :] = _dot(ckv, wk_ref[...]).astype(BF16)
        v_ref[rows, :] = _dot(ckv, wv_ref[...]).astype(BF16)
        pe2 = jnp.where(low, pe, jnp.where(lane < MLA_ROPE + half, -pltpu.roll(pe, half, axis=1),
                                           pltpu.roll(pe, V7X_LANES - half, axis=1)))
        kpe_ref[rows, :] = rope(pe2).astype(BF16)
        for hd in range(MLA_HEADS):
            qf_ref[rows, hd * P:hd * P + MLA_NOPE] = (qf[:, hd * P:hd * P + MLA_NOPE] * scale).astype(BF16)
            qf_ref[rows, hd * P + MLA_NOPE:(hd + 1) * P] = (
                rope(qf[:, hd * P + MLA_NOPE:(hd + 1) * P]) * scale).astype(BF16)


def _inproj(x, mod_l, g, w, tab, qg, kg, wq, wk, wv, layer, seq):
    T, D = x.shape
    n_hg = 4 * HG_HEADS * HG_DIM
    N = w.shape[-1] - n_hg
    assert n_hg % N == 0
    col_block = n_hg // N
    tm = _tile(seq, 1024)
    rb = _tile(tm, 512)
    per_seq = seq // tm
    H = MLA_HEADS
    row = lambda width: pl.BlockSpec((tm, width), lambda i: (i, 0))
    full = lambda a: pl.BlockSpec(a.shape, lambda i: (0, 0), pipeline_mode=pl.Buffered(1))
    widths = (D, H * MLA_QK_PAD, H * MLA_NOPE, V7X_LANES, H * MLA_V)
    return pl.pallas_call(
        functools.partial(_inproj_kernel, sub=1, tm=tm, rb=rb),
        grid=(T // tm,),
        in_specs=[
            row(D),
            pl.BlockSpec((None, 3 * N_SUB, D), lambda i: (i // per_seq, 0, 0)),
            pl.BlockSpec((1, D), lambda i: (0, 0)),
            pl.BlockSpec((None, D, N), lambda i: (layer, 0, col_block), pipeline_mode=pl.Buffered(1)),
            row(V7X_LANES), full(qg), full(kg), full(wq), full(wk), full(wv),
        ],
        out_specs=[row(wd) for wd in widths],
        out_shape=[jax.ShapeDtypeStruct((T, wd), BF16) for wd in widths],
        compiler_params=_params("parallel"),
    )(x, mod_l, g, w, tab, qg, kg, wq, wk, wv)


def _hgrn_head(hq, z, hi, hg, lb, gn, st, *, ts):
    C = HG_CHUNK
    nc = ts // C
    u = jnp.exp(-jnp.abs(z))
    r = 1.0 / (1.0 + u)
    ur = u * r
    z_pos = z >= 0.0
    log_f = jnp.log(lb + (1.0 - lb) * jnp.where(z_pos, r, ur))
    k = (1.0 - lb) * jnp.where(z_pos, ur, r)
    q = _silu(hq)

    rows8 = 8
    per_chunk = C // rows8
    x = log_f.reshape(ts // rows8, rows8, HG_DIM)
    sub = lax.broadcasted_iota(jnp.int32, (1, rows8, HG_DIM), 1)
    step = 1
    while step < rows8:
        x = x + jnp.where(sub >= step, pltpu.roll(x, step, axis=1), 0.0)
        step *= 2
    x = x.reshape(nc, per_chunk, rows8, HG_DIM)
    parts = [x[:, 0]]
    carry = x[:, 0, rows8 - 1:rows8, :]
    for i in range(1, per_chunk):
        parts.append(x[:, i] + carry)
        if i + 1 < per_chunk:
            carry = carry + x[:, i, rows8 - 1:rows8, :]
    b3 = jnp.concatenate(parts, axis=1)

    b_end = b3[:, C - 1:C, :]
    q3 = q.reshape(nc, C, HG_DIM)
    k3 = k.reshape(nc, C, HG_DIM)
    decay = jnp.exp(b_end)
    q_in = (q3 * jnp.exp(b3)).astype(BF16).reshape(ts, HG_DIM)
    k_back = k3 * jnp.exp(-b3)
    k_in = k_back.astype(BF16).reshape(ts, HG_DIM)
    k_out = (k_back * decay).astype(BF16).reshape(ts, HG_DIM)
    v = hi.astype(BF16)

    G = HG_GROUP
    R = G * C
    ri = lax.broadcasted_iota(jnp.int32, (R, R), 0)
    ci = lax.broadcasted_iota(jnp.int32, (R, R), 1)
    causal = (ci <= ri) & ((ri // C) == (ci // C))
    zeros = jnp.zeros((C, HG_DIM), BF16)

    def block_arranged(m):
        return jnp.concatenate(
            [jnp.concatenate([m[c * C:(c + 1) * C] if cc == c else zeros for cc in range(G)], axis=1)
             for c in range(G)], axis=0)

    outs = []
    for g in range(ts // R):
        sl = slice(g * R, (g + 1) * R)
        a = jnp.where(causal, _dot_nt(q_in[sl], k_in[sl]), 0.0)
        o = _dot(a.astype(BF16), v[sl])
        kv = _dot_tn(v[sl], block_arranged(k_out[sl]))
        starts = []
        for c in range(G):
            starts.append(st)
            st = st * decay[g * G + c] + kv[:, c * HG_DIM:(c + 1) * HG_DIM]
        st_all = jnp.concatenate(starts, axis=1).astype(BF16)
        outs.append(o + _dot_nt(block_arranged(q_in[sl]), st_all))
    o = jnp.concatenate(outs, axis=0)
    return (_rms(o, gn) * _silu(hg)).astype(BF16), st


def _hgrn_kernel(h_ref, wq_ref, wf_ref, wi_ref, wg_ref, lb_ref, gn_ref, o_ref, st_ref, *, ts, heads):
    @pl.when(pl.program_id(2) == 0)
    def _():
        st_ref[...] = jnp.zeros_like(st_ref)

    rb = min(ts, HG_ROW_BLOCK)
    states = [st_ref[h] for h in range(heads)]
    for r in range(ts // rb):
        rows = slice(r * rb, (r + 1) * rb)
        hb = h_ref[rows, :]
        ys = [_dot(hb, w[...]) for w in (wq_ref, wf_ref, wi_ref, wg_ref)]
        for h in range(heads):
            sl = slice(h * HG_DIM, (h + 1) * HG_DIM)
            hq, z, hi, hg = (y[:, sl] for y in ys)
            o, states[h] = _hgrn_head(hq, z, hi, hg, lb_ref[:, sl], gn_ref[...], states[h], ts=rb)
            o_ref[rows, sl] = o
    for h in range(heads):
        st_ref[h] = states[h]


def _hgrn(h, w, lb, gn, layer, batch, seq):
    T, D = h.shape
    ts = _tile(seq, 2048)
    ns = seq // ts
    hp = HG_HEADS_PER_STEP
    nh = HG_HEADS // hp
    wo = hp * HG_DIM

    def wcol(group):
        return pl.BlockSpec((None, D, wo), lambda b, p, s: (layer, 0, group * nh + p))

    return pl.pallas_call(
        functools.partial(_hgrn_kernel, ts=ts, heads=hp),
        grid=(batch, nh, ns),
        in_specs=[pl.BlockSpec((ts, D), lambda b, p, s: (b * ns + s, 0)),
                  wcol(0), wcol(1), wcol(2), wcol(3),
                  pl.BlockSpec((1, wo), lambda b, p, s: (0, p)),
                  pl.BlockSpec((1, HG_DIM), lambda b, p, s: (0, 0))],
        out_specs=pl.BlockSpec((ts, wo), lambda b, p, s: (b * ns + s, p)),
        out_shape=jax.ShapeDtypeStruct((T, HG_HEADS * HG_DIM), BF16),
        scratch_shapes=[pltpu.VMEM((hp, HG_DIM, HG_DIM), F32)],
        compiler_params=_params("parallel", "arbitrary", "arbitrary"),
    )(h, w, w, w, w, lb, gn)


def _attn_kernel(q_ref, kn_ref, kpe_ref, v_ref, o_ref, *, seq, t):
    n = seq // t
    k_all = jnp.concatenate([kn_ref[...], kpe_ref[...]], axis=1)
    keep = (lax.broadcasted_iota(jnp.int32, (t, t), 1)
            <= lax.broadcasted_iota(jnp.int32, (t, t), 0))

    def update(state, s, v):
        row_max = jnp.max(s, axis=-1, keepdims=True)
        if state is None:
            p = jnp.exp(s - row_max)
            return row_max, jnp.sum(p, axis=-1, keepdims=True), _dot(p.astype(BF16), v)
        m, l, acc = state
        m_new = jnp.maximum(m, row_max)
        alpha = jnp.exp(m - m_new)
        p = jnp.exp(s - m_new)
        return (m_new, alpha * l + jnp.sum(p, axis=-1, keepdims=True),
                alpha * acc + _dot(p.astype(BF16), v))

    for i in range(n):
        q = q_ref[i * t:(i + 1) * t, :]
        state = None
        for j in range(i + 1):
            s = _dot_nt(q, k_all[j * t:(j + 1) * t, :])
            if j == i:
                s = jnp.where(keep, s, -jnp.inf)
            state = update(state, s, v_ref[j * t:(j + 1) * t, :])
        _, l, acc = state
        o_ref[i * t:(i + 1) * t, :] = (acc / l).astype(BF16)


def _attn(qf, kn, kpe, v, batch, seq):
    T = qf.shape[0]
    t = _tile(seq, 512)
    H = MLA_HEADS
    return pl.pallas_call(
        functools.partial(_attn_kernel, seq=seq, t=t),
        grid=(batch, H),
        in_specs=[
            pl.BlockSpec((seq, MLA_QK_PAD), lambda b, h: (b, h)),
            pl.BlockSpec((seq, MLA_NOPE), lambda b, h: (b, h)),
            pl.BlockSpec((seq, V7X_LANES), lambda b, h: (b, 0)),
            pl.BlockSpec((seq, MLA_V), lambda b, h: (b, h)),
        ],
        out_specs=pl.BlockSpec((seq, MLA_V), lambda b, h: (b, h)),
        out_shape=jax.ShapeDtypeStruct((T, H * MLA_V), BF16),
        compiler_params=_params("parallel", "parallel"),
    )(qf, kn, kpe, v)


def _outproj_kernel(x_ref, mod_ref, a_ref, b_ref, wa_ref, wb_ref, o_ref, *, sub):
    y = _dot(a_ref[...], wa_ref[...]) + _dot(b_ref[...], wb_ref[...])
    o_ref[...] = x_ref[...] + mod_ref[3 * sub + 2:3 * sub + 3, :] * y


def _outproj(x, mod_l, a, b, wa, wb, seq):
    T, D = x.shape
    tm = _tile(seq, 512)
    per_seq = seq // tm
    return pl.pallas_call(
        functools.partial(_outproj_kernel, sub=1),
        grid=(T // tm,),
        in_specs=[
            pl.BlockSpec((tm, D), lambda i: (i, 0)),
            pl.BlockSpec((None, 3 * N_SUB, D), lambda i: (i // per_seq, 0, 0)),
            pl.BlockSpec((tm, a.shape[1]), lambda i: (i, 0)),
            pl.BlockSpec((tm, b.shape[1]), lambda i: (i, 0)),
            pl.BlockSpec(wa.shape, lambda i: (0, 0)),
            pl.BlockSpec(wb.shape, lambda i: (0, 0)),
        ],
        out_specs=pl.BlockSpec((tm, D), lambda i: (i, 0)),
        out_shape=jax.ShapeDtypeStruct((T, D), F32),
        compiler_params=_params("parallel"),
    )(x, mod_l, a, b, wa, wb)


def _rotate_half_cols(w):
    half = MLA_ROPE // 2
    w1, w2 = w[..., :half], w[..., half:]
    return jnp.concatenate([-w2, w1], axis=-1)


def _prep_w_in(w_in):
    pad = (-w_in.shape[-1]) % (4 * V7X_MXU_DIM)
    return jnp.pad(w_in.astype(BF16), ((0, 0), (0, 0), (0, pad)))


def _prep_w_q(w_q_up):
    L, R, _ = w_q_up.shape
    w = w_q_up.reshape(L, R, MLA_HEADS, MLA_NOPE + MLA_ROPE)
    pe = w[..., MLA_NOPE:]
    w = jnp.concatenate([w, _rotate_half_cols(pe)], axis=-1)
    return w.reshape(L, R, MLA_HEADS * MLA_QK_PAD).astype(BF16)


def _prep_w_kv(w_kv_up):
    L, R, _ = w_kv_up.shape
    w = w_kv_up.reshape(L, R, MLA_HEADS, MLA_NOPE + MLA_V)
    wk = w[..., :MLA_NOPE].reshape(L, R, MLA_HEADS * MLA_NOPE)
    wv = w[..., MLA_NOPE:].reshape(L, R, MLA_HEADS * MLA_V)
    return wk.astype(BF16), wv.astype(BF16)


def kernel(x, c, positions, w_ada, b_ada, norm_g, w_in, qa_norm_g, w_q_up, kva_norm_g, w_kv_up,
           hg_lb_logits, hg_norm_g, w_out, ffn_w_gate, ffn_w_up, ffn_w_down, final_norm_g):
    B, S, D = x.shape
    L = w_ada.shape[0]
    T = B * S

    half = MLA_ROPE // 2
    inv_freq = ROPE_THETA ** (-jnp.arange(half, dtype=F32) / half)
    ang = positions.astype(F32).reshape(T, 1) * inv_freq
    cos, sin = jnp.cos(ang), jnp.sin(ang)
    tab = jnp.concatenate([cos, cos, sin, sin], axis=-1)

    lb_all = jnp.cumsum(jax.nn.softmax(hg_lb_logits.astype(F32), axis=0), axis=0)
    lb_all = lb_all - lb_all[0:1]

    w_in_p = _prep_w_in(w_in)
    w_q_p = _prep_w_q(w_q_up)
    w_k_p, w_v_p = _prep_w_kv(w_kv_up)
    hg_w = HG_HEADS * HG_DIM
    w_out_a = w_out[:, :hg_w, :].astype(BF16)
    w_out_b = w_out[:, hg_w:, :].astype(BF16)
    wg, wu, wd = ffn_w_gate.astype(BF16), ffn_w_up.astype(BF16), ffn_w_down

    mod = _ada(c, w_ada, b_ada).reshape(L, B, 3 * N_SUB, D)

    fin_g = final_norm_g.reshape(1, D)
    xf = x.reshape(T, D)
    for l in range(L):
        mod_l = mod[l]
        xf = _ffn(xf, mod_l, norm_g[l, 0:1], wg, wu, wd, fin_g, l, 0, 0, S, False)
        h, qf, kn, kpe, v = _inproj(xf, mod_l, norm_g[l, 1:2], w_in_p, tab, qa_norm_g[l:l + 1],
                                    kva_norm_g[l:l + 1], w_q_p[l], w_k_p[l], w_v_p[l], l, S)
        o_hg = _hgrn(h, w_in_p, lb_all[l:l + 1], hg_norm_g[l:l + 1], l, B, S)
        o_mla = _attn(qf, kn, kpe, v, B, S)
        xf = _outproj(xf, mod_l, o_hg, o_mla, w_out_a[l], w_out_b[l], S)
        xf = _ffn(xf, mod_l, norm_g[l, 2:3], wg, wu, wd, fin_g, l, 1, 2, S, l == L - 1)
    return xf.reshape(B, S, D)
```

```python
import functools

import jax
import jax.numpy as jnp
from jax import lax
from jax.experimental import pallas as pl
from jax.experimental.pallas import tpu as pltpu

EPS = 1e-6
N_SUB = 3
HG_HEADS = 8
HG_DIM = 128
HG_CHUNK = 32
HG_GROUP = 4
HG_HEADS_PER_STEP = 4
HG_ROW_BLOCK = 512
MLA_HEADS = 8
MLA_NOPE = 128
MLA_ROPE = 64
MLA_V = 128
MLA_Q_RANK = 512
MLA_KV_RANK = 256
ROPE_THETA = 10000.0
MLA_QK_PAD = 256
FFN_DOWN_SLAB = 512
FFN_TM = 1024

V7X_LANES = 128
V7X_MXU_DIM = 256
V7X_VMEM_BYTES = 64 * 1024 * 1024
VMEM_LIMIT = V7X_VMEM_BYTES - 1 * 1024 * 1024

F32 = jnp.float32
BF16 = jnp.bfloat16


def _params(*sem):
    return pltpu.CompilerParams(dimension_semantics=sem, vmem_limit_bytes=VMEM_LIMIT)


def _sigmoid(x):
    return 1.0 / (1.0 + jnp.exp(-x))


def _silu(x):
    return x * _sigmoid(x)


def _rms(x, g):
    ms = jnp.mean(x * x, axis=-1, keepdims=True)
    return (x * lax.rsqrt(ms + EPS)) * g


def _modulated_norm(x, g, shift, scale):
    return _rms(x, g) * (1.0 + scale) + shift


def _dot(a, b):
    return jnp.dot(a, b, preferred_element_type=F32)


def _dot_nt(a, b):
    return lax.dot_general(a, b, (((1,), (1,)), ((), ())), preferred_element_type=F32)


def _dot_tn(a, b):
    return lax.dot_general(a, b, (((0,), (0,)), ((), ())), preferred_element_type=F32)


def _tile(n, pref):
    t = min(n, pref)
    assert n % t == 0, (n, t)
    return t


def _ada_kernel(c_ref, w_ref, b_ref, o_ref):
    ca = _silu(c_ref[...]).astype(BF16)
    o_ref[...] = _dot(ca, w_ref[...].astype(BF16)) + b_ref[...]


def _ada(c, w_ada, b_ada):
    L, D, N = w_ada.shape
    B = c.shape[0]
    tn = _tile(N, 1024)
    return pl.pallas_call(
        _ada_kernel,
        grid=(L, N // tn),
        in_specs=[
            pl.BlockSpec((B, D), lambda l, j: (0, 0)),
            pl.BlockSpec((None, D, tn), lambda l, j: (l, 0, j)),
            pl.BlockSpec((None, 1, tn), lambda l, j: (l, 0, j)),
        ],
        out_specs=pl.BlockSpec((None, B, tn), lambda l, j: (l, 0, j)),
        out_shape=jax.ShapeDtypeStruct((L, B, N), F32),
        compiler_params=_params("parallel", "parallel"),
    )(c, w_ada, b_ada.reshape(L, 1, N))


def _ffn_kernel(x_hbm, mod_ref, modn_ref, g_ref, wg_ref, wu_ref, wd_ref, fin_ref, o_ref,
                xbuf, h_cur, h_next, sem, *, sub, tm, rows, n_slices, final):
    i = pl.program_id(0)
    j = pl.program_id(1)
    has_next = i + 1 < pl.num_programs(0)
    shift, scale, gate_row = 3 * sub, 3 * sub + 1, 3 * sub + 2

    def fetch(tile):
        return pltpu.make_async_copy(x_hbm.at[pl.ds(tile * tm, tm), :], xbuf.at[pl.ds(0, tm), :], sem)

    def norm_rows(r0, nrows, mod_src):
        h = _modulated_norm(xbuf[pl.ds(r0, nrows), :], g_ref[...],
                            mod_src[shift:shift + 1, :], mod_src[scale:scale + 1, :])
        h_next[pl.ds(r0, nrows), :] = h.astype(BF16)
        return h

    @pl.when((i == 0) & (j == 0))
    def _():
        first = fetch(0)
        first.start()
        xbuf[pl.ds(tm, rows), :] = jnp.zeros((rows, xbuf.shape[1]), F32)
        first.wait()
        for s in range(n_slices):
            norm_rows(s * rows, rows, mod_ref)

    @pl.when(j == 0)
    def _():
        o_ref[...] = xbuf[pl.ds(0, tm), :]
        h_cur[...] = h_next[pl.ds(0, tm), :]

    @pl.when((j == 2) & has_next)
    def _():
        fetch(i + 1).wait()

    scheduled = (j >= 2) & (j < 2 + n_slices)
    r0 = pl.multiple_of(jnp.where(scheduled, (j - 2) * rows, tm), rows)
    hn = norm_rows(r0, rows, modn_ref)
    d_model = o_ref.shape[1]
    tf = wg_ref.shape[1]
    bits = pltpu.bitcast(hn, jnp.uint32).reshape(rows // 8, 8, d_model)
    folded = bits[0]
    for r in range(1, rows // 8):
        folded = folded | bits[r]
    lanes = folded[:, 0:tf]
    for k in range(1, d_model // tf):
        lanes = lanes | folded[:, k * tf:(k + 1) * tf]
    half_word = jnp.uint32(16)
    zero = pltpu.bitcast(lax.shift_right_logical(lax.shift_right_logical(lanes, half_word), half_word), F32)

    h = h_cur[...]
    gate = _dot(h, wg_ref[...])
    up = (_dot(h, wu_ref[...].astype(BF16)).reshape(tm // 8, 8, tf) + zero).reshape(tm, tf)
    a = (_silu(gate) * up).astype(BF16)
    c = 0.5 * mod_ref[gate_row:gate_row + 1, :]
    slab = min(d_model, FFN_DOWN_SLAB)
    for n in range(d_model // slab):
        sl = slice(n * slab, (n + 1) * slab)
        o_ref[:, sl] += c[:, sl] * _dot(a, wd_ref[:, sl].astype(BF16))

    @pl.when((j == 0) & has_next)
    def _():
        fetch(i + 1).start()

    if final:
        @pl.when(j == pl.num_programs(1) - 1)
        def _():
            o_ref[...] = _rms(o_ref[...], fin_ref[...])


def _ffn(x, mod_l, g, wg, wu, wd, fin_g, layer, which, sub, seq, final):
    T, D = x.shape
    F = wg.shape[-1]
    tm = _tile(seq, FFN_TM)
    tf = _tile(F, 512)
    nt, nf = T // tm, F // tf
    per_seq = seq // tm
    n_slices = 1
    while 2 * n_slices <= min(nf - 2, 8):
        n_slices *= 2
    assert nf >= 2 + n_slices and tm % (16 * n_slices) == 0, (nf, tm)
    rows = tm // n_slices
    mod_spec = lambda tile_of: pl.BlockSpec((None, 3 * N_SUB, D), lambda i, j: (tile_of(i) // per_seq, 0, 0))
    return pl.pallas_call(
        functools.partial(_ffn_kernel, sub=sub, tm=tm, rows=rows, n_slices=n_slices, final=final),
        grid=(nt, nf),
        in_specs=[
            pl.BlockSpec(memory_space=pl.ANY),
            mod_spec(lambda i: i),
            mod_spec(lambda i: jnp.minimum(i + 1, nt - 1)),
            pl.BlockSpec((1, D), lambda i, j: (0, 0)),
            pl.BlockSpec((None, None, D, tf), lambda i, j: (layer, which, 0, j)),
            pl.BlockSpec((None, None, D, tf), lambda i, j: (layer, which, 0, j)),
            pl.BlockSpec((None, None, tf, D), lambda i, j: (layer, which, j, 0)),
            pl.BlockSpec((1, D), lambda i, j: (0, 0)),
        ],
        out_specs=pl.BlockSpec((tm, D), lambda i, j: (i, 0)),
        out_shape=jax.ShapeDtypeStruct((T, D), F32),
        scratch_shapes=[pltpu.VMEM((tm + rows, D), F32), pltpu.VMEM((tm, D), BF16),
                        pltpu.VMEM((tm + rows, D), BF16), pltpu.SemaphoreType.DMA(())],
        compiler_params=_params("arbitrary", "arbitrary"),
    )(x, mod_l, mod_l, g, wg, wu, wd, fin_g)


def _inproj_kernel(x_ref, mod_ref, g_ref, w_ref, tab_ref, qg_ref, kg_ref, wq_ref, wk_ref, wv_ref,
                   h_ref, qf_ref, kn_ref, kpe_ref, v_ref, *, sub, tm, rb):
    scale = (MLA_NOPE + MLA_ROPE) ** -0.5
    lane = lax.broadcasted_iota(jnp.int32, (rb, V7X_LANES), 1)
    low = lane < MLA_ROPE
    half = MLA_ROPE // 2
    P = MLA_QK_PAD

    for r in range(tm // rb):
        rows = slice(r * rb, (r + 1) * rb)
        tab = tab_ref[rows, :]

        def rope(y2):
            y = y2 * tab
            return jnp.where(low, y + pltpu.roll(y, MLA_ROPE, axis=1), 0.0)

        h = _modulated_norm(x_ref[rows, :], g_ref[...], mod_ref[3 * sub:3 * sub + 1, :],
                            mod_ref[3 * sub + 1:3 * sub + 2, :]).astype(BF16)
        h_ref[rows, :] = h
        proj = _dot(h, w_ref[...])
        qa = proj[:, :MLA_Q_RANK]
        kva = proj[:, MLA_Q_RANK:MLA_Q_RANK + MLA_KV_RANK]
        pe = proj[:, MLA_Q_RANK + MLA_KV_RANK:MLA_Q_RANK + MLA_KV_RANK + V7X_LANES]

        cq = _rms(qa, qg_ref[...]).astype(BF16)
        qf = _dot(cq, wq_ref[...])
        ckv = _rms(kva, kg_ref[...]).astype(BF16)
        kn_ref[rows, :] = _dot(ckv, wk_ref[...]).astype(BF16)
        v_ref[rows, :] = _dot(ckv, wv_ref[...]).astype(BF16)
        pe2 = jnp.where(low, pe, jnp.where(lane < MLA_ROPE + half, -pltpu.roll(pe, half, axis=1),
                                           pltpu.roll(pe, V7X_LANES - half, axis=1)))
        kpe_ref[rows, :] = rope(pe2).astype(BF16)
        for hd in range(MLA_HEADS):
            qf_ref[rows, hd * P:hd * P + MLA_NOPE] = (qf[:, hd * P:hd * P + MLA_NOPE] * scale).astype(BF16)
            qf_ref[rows, hd * P + MLA_NOPE:(hd + 1) * P] = (
                rope(qf[:, hd * P + MLA_NOPE:(hd + 1) * P]) * scale).astype(BF16)


def _inproj(x, mod_l, g, w, tab, qg, kg, wq, wk, wv, layer, seq):
    T, D = x.shape
    n_hg = 4 * HG_HEADS * HG_DIM
    N = w.shape[-1] - n_hg
    assert n_hg % N == 0
    col_block = n_hg // N
    tm = _tile(seq, 1024)
    rb = _tile(tm, 512)
    per_seq = seq // tm
    H = MLA_HEADS
    row = lambda width: pl.BlockSpec((tm, width), lambda i: (i, 0))
    full = lambda a: pl.BlockSpec(a.shape, lambda i: (0, 0), pipeline_mode=pl.Buffered(1))
    widths = (D, H * MLA_QK_PAD, H * MLA_NOPE, V7X_LANES, H * MLA_V)
    return pl.pallas_call(
        functools.partial(_inproj_kernel, sub=1, tm=tm, rb=rb),
        grid=(T // tm,),
        in_specs=[
            row(D),
            pl.BlockSpec((None, 3 * N_SUB, D), lambda i: (i // per_seq, 0, 0)),
            pl.BlockSpec((1, D), lambda i: (0, 0)),
            pl.BlockSpec((None, D, N), lambda i: (layer, 0, col_block), pipeline_mode=pl.Buffered(1)),
            row(V7X_LANES), full(qg), full(kg), full(wq), full(wk), full(wv),
        ],
        out_specs=[row(wd) for wd in widths],
        out_shape=[jax.ShapeDtypeStruct((T, wd), BF16) for wd in widths],
        compiler_params=_params("parallel"),
    )(x, mod_l, g, w, tab, qg, kg, wq, wk, wv)


def _hgrn_head(hq, z, hi, hg, lb, gn, st, *, ts):
    C = HG_CHUNK
    nc = ts // C
    u = jnp.exp(-jnp.abs(z))
    r = 1.0 / (1.0 + u)
    ur = u * r
    z_pos = z >= 0.0
    log_f = jnp.log(lb + (1.0 - lb) * jnp.where(z_pos, r, ur))
    k = (1.0 - lb) * jnp.where(z_pos, ur, r)
    q = _silu(hq)

    rows8 = 8
    per_chunk = C // rows8
    x = log_f.reshape(ts // rows8, rows8, HG_DIM)
    sub = lax.broadcasted_iota(jnp.int32, (1, rows8, HG_DIM), 1)
    step = 1
    while step < rows8:
        x = x + jnp.where(sub >= step, pltpu.roll(x, step, axis=1), 0.0)
        step *= 2
    x = x.reshape(nc, per_chunk, rows8, HG_DIM)
    parts = [x[:, 0]]
    carry = x[:, 0, rows8 - 1:rows8, :]
    for i in range(1, per_chunk):
        parts.append(x[:, i] + carry)
        if i + 1 < per_chunk:
            carry = carry + x[:, i, rows8 - 1:rows8, :]
    b3 = jnp.concatenate(parts, axis=1)

    b_end = b3[:, C - 1:C, :]
    q3 = q.reshape(nc, C, HG_DIM)
    k3 = k.reshape(nc, C, HG_DIM)
    decay = jnp.exp(b_end)
    q_in = (q3 * jnp.exp(b3)).astype(BF16).reshape(ts, HG_DIM)
    k_back = k3 * jnp.exp(-b3)
    k_in = k_back.astype(BF16).reshape(ts, HG_DIM)
    k_out = (k_back * decay).astype(BF16).reshape(ts, HG_DIM)
    v = hi.astype(BF16)

    G = HG_GROUP
    R = G * C
    ri = lax.broadcasted_iota(jnp.int32, (R, R), 0)
    ci = lax.broadcasted_iota(jnp.int32, (R, R), 1)
    causal = (ci <= ri) & ((ri // C) == (ci // C))
    zeros = jnp.zeros((C, HG_DIM), BF16)

    def block_arranged(m):
        return jnp.concatenate(
            [jnp.concatenate([m[c * C:(c + 1) * C] if cc == c else zeros for cc in range(G)], axis=1)
             for c in range(G)], axis=0)

    outs = []
    for g in range(ts // R):
        sl = slice(g * R, (g + 1) * R)
        a = jnp.where(causal, _dot_nt(q_in[sl], k_in[sl]), 0.0)
        o = _dot(a.astype(BF16), v[sl])
        kv = _dot_tn(v[sl], block_arranged(k_out[sl]))
        starts = []
        for c in range(G):
            starts.append(st)
            st = st * decay[g * G + c] + kv[:, c * HG_DIM:(c + 1) * HG_DIM]
        st_all = jnp.concatenate(starts, axis=1).astype(BF16)
        outs.append(o + _dot_nt(block_arranged(q_in[sl]), st_all))
    o = jnp.concatenate(outs, axis=0)
    return (_rms(o, gn) * _silu(hg)).astype(BF16), st


def _hgrn_kernel(h_ref, wq_ref, wf_ref, wi_ref, wg_ref, lb_ref, gn_ref, o_ref, st_ref, *, ts, heads):
    @pl.when(pl.program_id(2) == 0)
    def _():
        st_ref[...] = jnp.zeros_like(st_ref)

    rb = min(ts, HG_ROW_BLOCK)
    states = [st_ref[h] for h in range(heads)]
    for r in range(ts // rb):
        rows = slice(r * rb, (r + 1) * rb)
        hb = h_ref[rows, :]
        ys = [_dot(hb, w[...]) for w in (wq_ref, wf_ref, wi_ref, wg_ref)]
        for h in range(heads):
            sl = slice(h * HG_DIM, (h + 1) * HG_DIM)
            hq, z, hi, hg = (y[:, sl] for y in ys)
            o, states[h] = _hgrn_head(hq, z, hi, hg, lb_ref[:, sl], gn_ref[...], states[h], ts=rb)
            o_ref[rows, sl] = o
    for h in range(heads):
        st_ref[h] = states[h]


def _hgrn(h, w, lb, gn, layer, batch, seq):
    T, D = h.shape
    ts = _tile(seq, 2048)
    ns = seq // ts
    hp = HG_HEADS_PER_STEP
    nh = HG_HEADS // hp
    wo = hp * HG_DIM

    def wcol(group):
        return pl.BlockSpec((None, D, wo), lambda b, p, s: (layer, 0, group * nh + p))

    return pl.pallas_call(
        functools.partial(_hgrn_kernel, ts=ts, heads=hp),
        grid=(batch, nh, ns),
        in_specs=[pl.BlockSpec((ts, D), lambda b, p, s: (b * ns + s, 0)),
                  wcol(0), wcol(1), wcol(2), wcol(3),
                  pl.BlockSpec((1, wo), lambda b, p, s: (0, p)),
                  pl.BlockSpec((1, HG_DIM), lambda b, p, s: (0, 0))],
        out_specs=pl.BlockSpec((ts, wo), lambda b, p, s: (b * ns + s, p)),
        out_shape=jax.ShapeDtypeStruct((T, HG_HEADS * HG_DIM), BF16),
        scratch_shapes=[pltpu.VMEM((hp, HG_DIM, HG_DIM), F32)],
        compiler_params=_params("parallel", "arbitrary", "arbitrary"),
    )(h, w, w, w, w, lb, gn)


def _attn_kernel(q_ref, kn_ref, kpe_ref, v_ref, o_ref, *, seq, t):
    n = seq // t
    k_all = jnp.concatenate([kn_ref[...], kpe_ref[...]], axis=1)
    keep = (lax.broadcasted_iota(jnp.int32, (t, t), 1)
            <= lax.broadcasted_iota(jnp.int32, (t, t), 0))

    def update(state, s, v):
        row_max = jnp.max(s, axis=-1, keepdims=True)
        if state is None:
            p = jnp.exp(s - row_max)
            return row_max, jnp.sum(p, axis=-1, keepdims=True), _dot(p.astype(BF16), v)
        m, l, acc = state
        m_new = jnp.maximum(m, row_max)
        alpha = jnp.exp(m - m_new)
        p = jnp.exp(s - m_new)
        return (m_new, alpha * l + jnp.sum(p, axis=-1, keepdims=True),
                alpha * acc + _dot(p.astype(BF16), v))

    for i in range(n):
        q = q_ref[i * t:(i + 1) * t, :]
        state = None
        for j in range(i + 1):
            s = _dot_nt(q, k_all[j * t:(j + 1) * t, :])
            if j == i:
                s = jnp.where(keep, s, -jnp.inf)
            state = update(state, s, v_ref[j * t:(j + 1) * t, :])
        _, l, acc = state
        o_ref[i * t:(i + 1) * t, :] = (acc / l).astype(BF16)


def _attn(qf, kn, kpe, v, batch, seq):
    T = qf.shape[0]
    t = _tile(seq, 512)
    H = MLA_HEADS
    return pl.pallas_call(
        functools.partial(_attn_kernel, seq=seq, t=t),
        grid=(batch, H),
        in_specs=[
            pl.BlockSpec((seq, MLA_QK_PAD), lambda b, h: (b, h)),
            pl.BlockSpec((seq, MLA_NOPE), lambda b, h: (b, h)),
            pl.BlockSpec((seq, V7X_LANES), lambda b, h: (b, 0)),
            pl.BlockSpec((seq, MLA_V), lambda b, h: (b, h)),
        ],
        out_specs=pl.BlockSpec((seq, MLA_V), lambda b, h: (b, h)),
        out_shape=jax.ShapeDtypeStruct((T, H * MLA_V), BF16),
        compiler_params=_params("parallel", "parallel"),
    )(qf, kn, kpe, v)


def _outproj_kernel(x_ref, mod_ref, a_ref, b_ref, wa_ref, wb_ref, o_ref, *, sub):
    y = _dot(a_ref[...], wa_ref[...]) + _dot(b_ref[...], wb_ref[...])
    o_ref[...] = x_ref[...] + mod_ref[3 * sub + 2:3 * sub + 3, :] * y


def _outproj(x, mod_l, a, b, wa, wb, seq):
    T, D = x.shape
    tm = _tile(seq, 512)
    per_seq = seq // tm
    return pl.pallas_call(
        functools.partial(_outproj_kernel, sub=1),
        grid=(T // tm,),
        in_specs=[
            pl.BlockSpec((tm, D), lambda i: (i, 0)),
            pl.BlockSpec((None, 3 * N_SUB, D), lambda i: (i // per_seq, 0, 0)),
            pl.BlockSpec((tm, a.shape[1]), lambda i: (i, 0)),
            pl.BlockSpec((tm, b.shape[1]), lambda i: (i, 0)),
            pl.BlockSpec(wa.shape, lambda i: (0, 0)),
            pl.BlockSpec(wb.shape, lambda i: (0, 0)),
        ],
        out_specs=pl.BlockSpec((tm, D), lambda i: (i, 0)),
        out_shape=jax.ShapeDtypeStruct((T, D), F32),
        compiler_params=_params("parallel"),
    )(x, mod_l, a, b, wa, wb)


def _rotate_half_cols(w):
    half = MLA_ROPE // 2
    w1, w2 = w[..., :half], w[..., half:]
    return jnp.concatenate([-w2, w1], axis=-1)


def _prep_w_in(w_in):
    pad = (-w_in.shape[-1]) % (4 * V7X_MXU_DIM)
    return jnp.pad(w_in.astype(BF16), ((0, 0), (0, 0), (0, pad)))


def _prep_w_q(w_q_up):
    L, R, _ = w_q_up.shape
    w = w_q_up.reshape(L, R, MLA_HEADS, MLA_NOPE + MLA_ROPE)
    pe = w[..., MLA_NOPE:]
    w = jnp.concatenate([w, _rotate_half_cols(pe)], axis=-1)
    return w.reshape(L, R, MLA_HEADS * MLA_QK_PAD).astype(BF16)


def _prep_w_kv(w_kv_up):
    L, R, _ = w_kv_up.shape
    w = w_kv_up.reshape(L, R, MLA_HEADS, MLA_NOPE + MLA_V)
    wk = w[..., :MLA_NOPE].reshape(L, R, MLA_HEADS * MLA_NOPE)
    wv = w[..., MLA_NOPE:].reshape(L, R, MLA_HEADS * MLA_V)
    return wk.astype(BF16), wv.astype(BF16)


def kernel(x, c, positions, w_ada, b_ada, norm_g, w_in, qa_norm_g, w_q_up, kva_norm_g, w_kv_up,
           hg_lb_logits, hg_norm_g, w_out, ffn_w_gate, ffn_w_up, ffn_w_down, final_norm_g):
    B, S, D = x.shape
    L = w_ada.shape[0]
    T = B * S

    half = MLA_ROPE // 2
    inv_freq = ROPE_THETA ** (-jnp.arange(half, dtype=F32) / half)
    ang = positions.astype(F32).reshape(T, 1) * inv_freq
    cos, sin = jnp.cos(ang), jnp.sin(ang)
    tab = jnp.concatenate([cos, cos, sin, sin], axis=-1)

    lb_all = jnp.cumsum(jax.nn.softmax(hg_lb_logits.astype(F32), axis=0), axis=0)
    lb_all = lb_all - lb_all[0:1]

    w_in_p = _prep_w_in(w_in)
    w_q_p = _prep_w_q(w_q_up)
    w_k_p, w_v_p = _prep_w_kv(w_kv_up)
    hg_w = HG_HEADS * HG_DIM
    w_out_a = w_out[:, :hg_w, :].astype(BF16)
    w_out_b = w_out[:, hg_w:, :].astype(BF16)
    wg, wu, wd = ffn_w_gate.astype(BF16), ffn_w_up, ffn_w_down

    mod = _ada(c, w_ada, b_ada).reshape(L, B, 3 * N_SUB, D)

    fin_g = final_norm_g.reshape(1, D)
    xf = x.reshape(T, D)
    for l in range(L):
        mod_l = mod[l]
        xf = _ffn(xf, mod_l, norm_g[l, 0:1], wg, wu, wd, fin_g, l, 0, 0, S, False)
        h, qf, kn, kpe, v = _inproj(xf, mod_l, norm_g[l, 1:2], w_in_p, tab, qa_norm_g[l:l + 1],
                                    kva_norm_g[l:l + 1], w_q_p[l], w_k_p[l], w_v_p[l], l, S)
        o_hg = _hgrn(h, w_in_p, lb_all[l:l + 1], hg_norm_g[l:l + 1], l, B, S)
        o_mla = _attn(qf, kn, kpe, v, B, S)
        xf = _outproj(xf, mod_l, o_hg, o_mla, w_out_a[l], w_out_b[l], S)
        xf = _ffn(xf, mod_l, norm_g[l, 2:3], wg, wu, wd, fin_g, l, 1, 2, S, l == L - 1)
    return xf.reshape(B, S, D)
```

```python
import functools

import jax
import jax.numpy as jnp
from jax import lax
from jax.experimental import pallas as pl
from jax.experimental.pallas import tpu as pltpu

EPS = 1e-6
N_SUB = 3
HG_HEADS = 8
HG_DIM = 128
HG_CHUNK = 32
HG_GROUP = 4
HG_HEADS_PER_STEP = 4
HG_ROW_BLOCK = 512
MLA_HEADS = 8
MLA_NOPE = 128
MLA_ROPE = 64
MLA_V = 128
MLA_Q_RANK = 512
MLA_KV_RANK = 256
ROPE_THETA = 10000.0
MLA_QK_PAD = 256
FFN_DOWN_SLAB = 512
FFN_TM = 1024

V7X_LANES = 128
V7X_MXU_DIM = 256
V7X_VMEM_BYTES = 64 * 1024 * 1024
VMEM_LIMIT = V7X_VMEM_BYTES - 1 * 1024 * 1024

F32 = jnp.float32
BF16 = jnp.bfloat16


def _params(*sem):
    return pltpu.CompilerParams(dimension_semantics=sem, vmem_limit_bytes=VMEM_LIMIT)


def _sigmoid(x):
    return 1.0 / (1.0 + jnp.exp(-x))


def _silu(x):
    return x * _sigmoid(x)


def _rms(x, g):
    ms = jnp.mean(x * x, axis=-1, keepdims=True)
    return (x * lax.rsqrt(ms + EPS)) * g


def _modulated_norm(x, g, shift, scale):
    return _rms(x, g) * (1.0 + scale) + shift


def _dot(a, b):
    return jnp.dot(a, b, preferred_element_type=F32)


def _dot_nt(a, b):
    return lax.dot_general(a, b, (((1,), (1,)), ((), ())), preferred_element_type=F32)


def _dot_tn(a, b):
    return lax.dot_general(a, b, (((0,), (0,)), ((), ())), preferred_element_type=F32)


def _tile(n, pref):
    t = min(n, pref)
    assert n % t == 0, (n, t)
    return t


def _ada_kernel(c_ref, w_ref, b_ref, o_ref):
    ca = _silu(c_ref[...]).astype(BF16)
    o_ref[...] = _dot(ca, w_ref[...].astype(BF16)) + b_ref[...]


def _ada(c, w_ada, b_ada):
    L, D, N = w_ada.shape
    B = c.shape[0]
    tn = _tile(N, 1024)
    return pl.pallas_call(
        _ada_kernel,
        grid=(L, N // tn),
        in_specs=[
            pl.BlockSpec((B, D), lambda l, j: (0, 0)),
            pl.BlockSpec((None, D, tn), lambda l, j: (l, 0, j)),
            pl.BlockSpec((None, 1, tn), lambda l, j: (l, 0, j)),
        ],
        out_specs=pl.BlockSpec((None, B, tn), lambda l, j: (l, 0, j)),
        out_shape=jax.ShapeDtypeStruct((L, B, N), F32),
        compiler_params=_params("parallel", "parallel"),
    )(c, w_ada, b_ada.reshape(L, 1, N))


def _ffn_kernel(x_hbm, mod_ref, modn_ref, g_ref, wg_ref, wu_ref, wd_ref, fin_ref, o_ref,
                xbuf, h_cur, h_next, sem, *, sub, tm, rows, n_slices, final):
    i = pl.program_id(0)
    j = pl.program_id(1)
    has_next = i + 1 < pl.num_programs(0)
    shift, scale, gate_row = 3 * sub, 3 * sub + 1, 3 * sub + 2

    def fetch(tile):
        return pltpu.make_async_copy(x_hbm.at[pl.ds(tile * tm, tm), :], xbuf.at[pl.ds(0, tm), :], sem)

    def norm_rows(r0, nrows, mod_src):
        h = _modulated_norm(xbuf[pl.ds(r0, nrows), :], g_ref[...],
                            mod_src[shift:shift + 1, :], mod_src[scale:scale + 1, :])
        h_next[pl.ds(r0, nrows), :] = h.astype(BF16)
        return h

    @pl.when((i == 0) & (j == 0))
    def _():
        first = fetch(0)
        first.start()
        xbuf[pl.ds(tm, rows), :] = jnp.zeros((rows, xbuf.shape[1]), F32)
        first.wait()
        for s in range(n_slices):
            norm_rows(s * rows, rows, mod_ref)

    @pl.when(j == 0)
    def _():
        h_cur[...] = h_next[pl.ds(0, tm), :]

    @pl.when((j == 2) & has_next)
    def _():
        fetch(i + 1).wait()

    d_model = o_ref.shape[1]
    tf = wg_ref.shape[1]
    slab = min(d_model, FFN_DOWN_SLAB)

    def matmul_step(first):
        zero = None
        if not first:
            scheduled = (j >= 2) & (j < 2 + n_slices)
            r0 = pl.multiple_of(jnp.where(scheduled, (j - 2) * rows, tm), rows)
            hn = norm_rows(r0, rows, modn_ref)
            bits = pltpu.bitcast(hn, jnp.uint32).reshape(rows // 8, 8, d_model)
            folded = bits[0]
            for r in range(1, rows // 8):
                folded = folded | bits[r]
            lanes = folded[:, 0:tf]
            for k in range(1, d_model // tf):
                lanes = lanes | folded[:, k * tf:(k + 1) * tf]
            half_word = jnp.uint32(16)
            zero = pltpu.bitcast(
                lax.shift_right_logical(lax.shift_right_logical(lanes, half_word), half_word), F32)

        h = h_cur[...]
        gate = _dot(h, wg_ref[...])
        if zero is not None:
            gate = (gate.reshape(tm // 8, 8, tf) + zero).reshape(tm, tf)
        up = _dot(h, wu_ref[...].astype(BF16))
        a = (_silu(gate) * up).astype(BF16)
        c = 0.5 * mod_ref[gate_row:gate_row + 1, :]
        for n in range(d_model // slab):
            sl = slice(n * slab, (n + 1) * slab)
            y = c[:, sl] * _dot(a, wd_ref[:, sl].astype(BF16))
            base = xbuf[pl.ds(0, tm), sl] if first else o_ref[:, sl]
            o_ref[:, sl] = base + y

    pl.when(j == 0)(functools.partial(matmul_step, True))
    pl.when(j > 0)(functools.partial(matmul_step, False))

    @pl.when((j == 0) & has_next)
    def _():
        fetch(i + 1).start()

    if final:
        @pl.when(j == pl.num_programs(1) - 1)
        def _():
            o_ref[...] = _rms(o_ref[...], fin_ref[...])


def _ffn(x, mod_l, g, wg, wu, wd, fin_g, layer, which, sub, seq, final):
    T, D = x.shape
    F = wg.shape[-1]
    tm = _tile(seq, FFN_TM)
    tf = _tile(F, 512)
    nt, nf = T // tm, F // tf
    per_seq = seq // tm
    n_slices = 1
    while 2 * n_slices <= min(nf - 2, 8):
        n_slices *= 2
    assert nf >= 2 + n_slices and tm % (16 * n_slices) == 0, (nf, tm)
    rows = tm // n_slices
    mod_spec = lambda tile_of: pl.BlockSpec((None, 3 * N_SUB, D), lambda i, j: (tile_of(i) // per_seq, 0, 0))
    return pl.pallas_call(
        functools.partial(_ffn_kernel, sub=sub, tm=tm, rows=rows, n_slices=n_slices, final=final),
        grid=(nt, nf),
        in_specs=[
            pl.BlockSpec(memory_space=pl.ANY),
            mod_spec(lambda i: i),
            mod_spec(lambda i: jnp.minimum(i + 1, nt - 1)),
            pl.BlockSpec((1, D), lambda i, j: (0, 0)),
            pl.BlockSpec((None, None, D, tf), lambda i, j: (layer, which, 0, j)),
            pl.BlockSpec((None, None, D, tf), lambda i, j: (layer, which, 0, j)),
            pl.BlockSpec((None, None, tf, D), lambda i, j: (layer, which, j, 0)),
            pl.BlockSpec((1, D), lambda i, j: (0, 0)),
        ],
        out_specs=pl.BlockSpec((tm, D), lambda i, j: (i, 0)),
        out_shape=jax.ShapeDtypeStruct((T, D), F32),
        scratch_shapes=[pltpu.VMEM((tm + rows, D), F32), pltpu.VMEM((tm, D), BF16),
                        pltpu.VMEM((tm + rows, D), BF16), pltpu.SemaphoreType.DMA(())],
        compiler_params=_params("arbitrary", "arbitrary"),
    )(x, mod_l, mod_l, g, wg, wu, wd, fin_g)


def _inproj_kernel(x_ref, mod_ref, g_ref, w_ref, tab_ref, qg_ref, kg_ref, wq_ref, wk_ref, wv_ref,
                   h_ref, qf_ref, kn_ref, kpe_ref, v_ref, *, sub, tm, rb):
    scale = (MLA_NOPE + MLA_ROPE) ** -0.5
    lane = lax.broadcasted_iota(jnp.int32, (rb, V7X_LANES), 1)
    low = lane < MLA_ROPE
    half = MLA_ROPE // 2
    P = MLA_QK_PAD

    for r in range(tm // rb):
        rows = slice(r * rb, (r + 1) * rb)
        tab = tab_ref[rows, :]

        def rope(y2):
            y = y2 * tab
            return jnp.where(low, y + pltpu.roll(y, MLA_ROPE, axis=1), 0.0)

        h = _modulated_norm(x_ref[rows, :], g_ref[...], mod_ref[3 * sub:3 * sub + 1, :],
                            mod_ref[3 * sub + 1:3 * sub + 2, :]).astype(BF16)
        h_ref[rows, :] = h
        proj = _dot(h, w_ref[...])
        qa = proj[:, :MLA_Q_RANK]
        kva = proj[:, MLA_Q_RANK:MLA_Q_RANK + MLA_KV_RANK]
        pe = proj[:, MLA_Q_RANK + MLA_KV_RANK:MLA_Q_RANK + MLA_KV_RANK + V7X_LANES]

        cq = _rms(qa, qg_ref[...]).astype(BF16)
        qf = _dot(cq, wq_ref[...])
        ckv = _rms(kva, kg_ref[...]).astype(BF16)
        kn_ref[rows, :] = _dot(ckv, wk_ref[...]).astype(BF16)
        v_ref[rows, :] = _dot(ckv, wv_ref[...]).astype(BF16)
        pe2 = jnp.where(low, pe, jnp.where(lane < MLA_ROPE + half, -pltpu.roll(pe, half, axis=1),
                                           pltpu.roll(pe, V7X_LANES - half, axis=1)))
        kpe_ref[rows, :] = rope(pe2).astype(BF16)
        for hd in range(MLA_HEADS):
            qf_ref[rows, hd * P:hd * P + MLA_NOPE] = (qf[:, hd * P:hd * P + MLA_NOPE] * scale).astype(BF16)
            qf_ref[rows, hd * P + MLA_NOPE:(hd + 1) * P] = (
                rope(qf[:, hd * P + MLA_NOPE:(hd + 1) * P]) * scale).astype(BF16)


def _inproj(x, mod_l, g, w, tab, qg, kg, wq, wk, wv, layer, seq):
    T, D = x.shape
    n_hg = 4 * HG_HEADS * HG_DIM
    N = w.shape[-1] - n_hg
    assert n_hg % N == 0
    col_block = n_hg // N
    tm = _tile(seq, 1024)
    rb = _tile(tm, 512)
    per_seq = seq // tm
    H = MLA_HEADS
    row = lambda width: pl.BlockSpec((tm, width), lambda i: (i, 0))
    full = lambda a: pl.BlockSpec(a.shape, lambda i: (0, 0), pipeline_mode=pl.Buffered(1))
    widths = (D, H * MLA_QK_PAD, H * MLA_NOPE, V7X_LANES, H * MLA_V)
    return pl.pallas_call(
        functools.partial(_inproj_kernel, sub=1, tm=tm, rb=rb),
        grid=(T // tm,),
        in_specs=[
            row(D),
            pl.BlockSpec((None, 3 * N_SUB, D), lambda i: (i // per_seq, 0, 0)),
            pl.BlockSpec((1, D), lambda i: (0, 0)),
            pl.BlockSpec((None, D, N), lambda i: (layer, 0, col_block), pipeline_mode=pl.Buffered(1)),
            row(V7X_LANES), full(qg), full(kg), full(wq), full(wk), full(wv),
        ],
        out_specs=[row(wd) for wd in widths],
        out_shape=[jax.ShapeDtypeStruct((T, wd), BF16) for wd in widths],
        compiler_params=_params("parallel"),
    )(x, mod_l, g, w, tab, qg, kg, wq, wk, wv)


def _hgrn_head(hq, z, hi, hg, lb, gn, st, *, ts):
    C = HG_CHUNK
    nc = ts // C
    u = jnp.exp(-jnp.abs(z))
    r = 1.0 / (1.0 + u)
    ur = u * r
    z_pos = z >= 0.0
    log_f = jnp.log(lb + (1.0 - lb) * jnp.where(z_pos, r, ur))
    k = (1.0 - lb) * jnp.where(z_pos, ur, r)
    q = _silu(hq)

    rows8 = 8
    per_chunk = C // rows8
    x = log_f.reshape(ts // rows8, rows8, HG_DIM)
    sub = lax.broadcasted_iota(jnp.int32, (1, rows8, HG_DIM), 1)
    step = 1
    while step < rows8:
        x = x + jnp.where(sub >= step, pltpu.roll(x, step, axis=1), 0.0)
        step *= 2
    x = x.reshape(nc, per_chunk, rows8, HG_DIM)
    parts = [x[:, 0]]
    carry = x[:, 0, rows8 - 1:rows8, :]
    for i in range(1, per_chunk):
        parts.append(x[:, i] + carry)
        if i + 1 < per_chunk:
            carry = carry + x[:, i, rows8 - 1:rows8, :]
    b3 = jnp.concatenate(parts, axis=1)

    b_end = b3[:, C - 1:C, :]
    q3 = q.reshape(nc, C, HG_DIM)
    k3 = k.reshape(nc, C, HG_DIM)
    decay = jnp.exp(b_end)
    q_in = (q3 * jnp.exp(b3)).astype(BF16).reshape(ts, HG_DIM)
    k_back = k3 * jnp.exp(-b3)
    k_in = k_back.astype(BF16).reshape(ts, HG_DIM)
    k_out = (k_back * decay).astype(BF16).reshape(ts, HG_DIM)
    v = hi.astype(BF16)

    G = HG_GROUP
    R = G * C
    ri = lax.broadcasted_iota(jnp.int32, (R, R), 0)
    ci = lax.broadcasted_iota(jnp.int32, (R, R), 1)
    causal = (ci <= ri) & ((ri // C) == (ci // C))
    zeros = jnp.zeros((C, HG_DIM), BF16)

    def block_arranged(m):
        return jnp.concatenate(
            [jnp.concatenate([m[c * C:(c + 1) * C] if cc == c else zeros for cc in range(G)], axis=1)
             for c in range(G)], axis=0)

    outs = []
    for g in range(ts // R):
        sl = slice(g * R, (g + 1) * R)
        a = jnp.where(causal, _dot_nt(q_in[sl], k_in[sl]), 0.0)
        o = _dot(a.astype(BF16), v[sl])
        kv = _dot_tn(v[sl], block_arranged(k_out[sl]))
        starts = []
        for c in range(G):
            starts.append(st)
            st = st * decay[g * G + c] + kv[:, c * HG_DIM:(c + 1) * HG_DIM]
        st_all = jnp.concatenate(starts, axis=1).astype(BF16)
        outs.append(o + _dot_nt(block_arranged(q_in[sl]), st_all))
    o = jnp.concatenate(outs, axis=0)
    return (_rms(o, gn) * _silu(hg)).astype(BF16), st


def _hgrn_kernel(h_ref, wq_ref, wf_ref, wi_ref, wg_ref, lb_ref, gn_ref, o_ref, st_ref, *, ts, heads):
    @pl.when(pl.program_id(2) == 0)
    def _():
        st_ref[...] = jnp.zeros_like(st_ref)

    rb = min(ts, HG_ROW_BLOCK)
    states = [st_ref[h] for h in range(heads)]
    for r in range(ts // rb):
        rows = slice(r * rb, (r + 1) * rb)
        hb = h_ref[rows, :]
        ys = [_dot(hb, w[...]) for w in (wq_ref, wf_ref, wi_ref, wg_ref)]
        for h in range(heads):
            sl = slice(h * HG_DIM, (h + 1) * HG_DIM)
            hq, z, hi, hg = (y[:, sl] for y in ys)
            o, states[h] = _hgrn_head(hq, z, hi, hg, lb_ref[:, sl], gn_ref[...], states[h], ts=rb)
            o_ref[rows, sl] = o
    for h in range(heads):
        st_ref[h] = states[h]


def _hgrn(h, w, lb, gn, layer, batch, seq):
    T, D = h.shape
    ts = _tile(seq, 2048)
    ns = seq // ts
    hp = HG_HEADS_PER_STEP
    nh = HG_HEADS // hp
    wo = hp * HG_DIM

    def wcol(group):
        return pl.BlockSpec((None, D, wo), lambda b, p, s: (layer, 0, group * nh + p))

    return pl.pallas_call(
        functools.partial(_hgrn_kernel, ts=ts, heads=hp),
        grid=(batch, nh, ns),
        in_specs=[pl.BlockSpec((ts, D), lambda b, p, s: (b * ns + s, 0)),
                  wcol(0), wcol(1), wcol(2), wcol(3),
                  pl.BlockSpec((1, wo), lambda b, p, s: (0, p)),
                  pl.BlockSpec((1, HG_DIM), lambda b, p, s: (0, 0))],
        out_specs=pl.BlockSpec((ts, wo), lambda b, p, s: (b * ns + s, p)),
        out_shape=jax.ShapeDtypeStruct((T, HG_HEADS * HG_DIM), BF16),
        scratch_shapes=[pltpu.VMEM((hp, HG_DIM, HG_DIM), F32)],
        compiler_params=_params("parallel", "arbitrary", "arbitrary"),
    )(h, w, w, w, w, lb, gn)


def _attn_kernel(q_ref, kn_ref, kpe_ref, v_ref, o_ref, *, seq, t):
    n = seq // t
    k_all = jnp.concatenate([kn_ref[...], kpe_ref[...]], axis=1)
    keep = (lax.broadcasted_iota(jnp.int32, (t, t), 1)
            <= lax.broadcasted_iota(jnp.int32, (t, t), 0))

    def update(state, s, v):
        row_max = jnp.max(s, axis=-1, keepdims=True)
        if state is None:
            p = jnp.exp(s - row_max)
            return row_max, jnp.sum(p, axis=-1, keepdims=True), _dot(p.astype(BF16), v)
        m, l, acc = state
        m_new = jnp.maximum(m, row_max)
        alpha = jnp.exp(m - m_new)
        p = jnp.exp(s - m_new)
        return (m_new, alpha * l + jnp.sum(p, axis=-1, keepdims=True),
                alpha * acc + _dot(p.astype(BF16), v))

    for i in range(n):
        q = q_ref[i * t:(i + 1) * t, :]
        state = None
        for j in range(i + 1):
            s = _dot_nt(q, k_all[j * t:(j + 1) * t, :])
            if j == i:
                s = jnp.where(keep, s, -jnp.inf)
            state = update(state, s, v_ref[j * t:(j + 1) * t, :])
        _, l, acc = state
        o_ref[i * t:(i + 1) * t, :] = (acc / l).astype(BF16)


def _attn(qf, kn, kpe, v, batch, seq):
    T = qf.shape[0]
    t = _tile(seq, 512)
    H = MLA_HEADS
    return pl.pallas_call(
        functools.partial(_attn_kernel, seq=seq, t=t),
        grid=(batch, H),
        in_specs=[
            pl.BlockSpec((seq, MLA_QK_PAD), lambda b, h: (b, h)),
            pl.BlockSpec((seq, MLA_NOPE), lambda b, h: (b, h)),
            pl.BlockSpec((seq, V7X_LANES), lambda b, h: (b, 0)),
            pl.BlockSpec((seq, MLA_V), lambda b, h: (b, h)),
        ],
        out_specs=pl.BlockSpec((seq, MLA_V), lambda b, h: (b, h)),
        out_shape=jax.ShapeDtypeStruct((T, H * MLA_V), BF16),
        compiler_params=_params("parallel", "parallel"),
    )(qf, kn, kpe, v)


def _outproj_kernel(x_ref, mod_ref, a_ref, b_ref, wa_ref, wb_ref, o_ref, *, sub):
    y = _dot(a_ref[...], wa_ref[...]) + _dot(b_ref[...], wb_ref[...])
    o_ref[...] = x_ref[...] + mod_ref[3 * sub + 2:3 * sub + 3, :] * y


def _outproj(x, mod_l, a, b, wa, wb, seq):
    T, D = x.shape
    tm = _tile(seq, 512)
    per_seq = seq // tm
    return pl.pallas_call(
        functools.partial(_outproj_kernel, sub=1),
        grid=(T // tm,),
        in_specs=[
            pl.BlockSpec((tm, D), lambda i: (i, 0)),
            pl.BlockSpec((None, 3 * N_SUB, D), lambda i: (i // per_seq, 0, 0)),
            pl.BlockSpec((tm, a.shape[1]), lambda i: (i, 0)),
            pl.BlockSpec((tm, b.shape[1]), lambda i: (i, 0)),
            pl.BlockSpec(wa.shape, lambda i: (0, 0)),
            pl.BlockSpec(wb.shape, lambda i: (0, 0)),
        ],
        out_specs=pl.BlockSpec((tm, D), lambda i: (i, 0)),
        out_shape=jax.ShapeDtypeStruct((T, D), F32),
        compiler_params=_params("parallel"),
    )(x, mod_l, a, b, wa, wb)


def _rotate_half_cols(w):
    half = MLA_ROPE // 2
    w1, w2 = w[..., :half], w[..., half:]
    return jnp.concatenate([-w2, w1], axis=-1)


def _prep_w_in(w_in):
    pad = (-w_in.shape[-1]) % (4 * V7X_MXU_DIM)
    return jnp.pad(w_in.astype(BF16), ((0, 0), (0, 0), (0, pad)))


def _prep_w_q(w_q_up):
    L, R, _ = w_q_up.shape
    w = w_q_up.reshape(L, R, MLA_HEADS, MLA_NOPE + MLA_ROPE)
    pe = w[..., MLA_NOPE:]
    w = jnp.concatenate([w, _rotate_half_cols(pe)], axis=-1)
    return w.reshape(L, R, MLA_HEADS * MLA_QK_PAD).astype(BF16)


def _prep_w_kv(w_kv_up):
    L, R, _ = w_kv_up.shape
    w = w_kv_up.reshape(L, R, MLA_HEADS, MLA_NOPE + MLA_V)
    wk = w[..., :MLA_NOPE].reshape(L, R, MLA_HEADS * MLA_NOPE)
    wv = w[..., MLA_NOPE:].reshape(L, R, MLA_HEADS * MLA_V)
    return wk.astype(BF16), wv.astype(BF16)


def kernel(x, c, positions, w_ada, b_ada, norm_g, w_in, qa_norm_g, w_q_up, kva_norm_g, w_kv_up,
           hg_lb_logits, hg_norm_g, w_out, ffn_w_gate, ffn_w_up, ffn_w_down, final_norm_g):
    B, S, D = x.shape
    L = w_ada.shape[0]
    T = B * S

    half = MLA_ROPE // 2
    inv_freq = ROPE_THETA ** (-jnp.arange(half, dtype=F32) / half)
    ang = positions.astype(F32).reshape(T, 1) * inv_freq
    cos, sin = jnp.cos(ang), jnp.sin(ang)
    tab = jnp.concatenate([cos, cos, sin, sin], axis=-1)

    lb_all = jnp.cumsum(jax.nn.softmax(hg_lb_logits.astype(F32), axis=0), axis=0)
    lb_all = lb_all - lb_all[0:1]

    w_in_p = _prep_w_in(w_in)
    w_q_p = _prep_w_q(w_q_up)
    w_k_p, w_v_p = _prep_w_kv(w_kv_up)
    hg_w = HG_HEADS * HG_DIM
    w_out_a = w_out[:, :hg_w, :].astype(BF16)
    w_out_b = w_out[:, hg_w:, :].astype(BF16)
    wg, wu, wd = ffn_w_gate.astype(BF16), ffn_w_up, ffn_w_down

    mod = _ada(c, w_ada, b_ada).reshape(L, B, 3 * N_SUB, D)

    fin_g = final_norm_g.reshape(1, D)
    xf = x.reshape(T, D)
    for l in range(L):
        mod_l = mod[l]
        xf = _ffn(xf, mod_l, norm_g[l, 0:1], wg, wu, wd, fin_g, l, 0, 0, S, False)
        h, qf, kn, kpe, v = _inproj(xf, mod_l, norm_g[l, 1:2], w_in_p, tab, qa_norm_g[l:l + 1],
                                    kva_norm_g[l:l + 1], w_q_p[l], w_k_p[l], w_v_p[l], l, S)
        o_hg = _hgrn(h, w_in_p, lb_all[l:l + 1], hg_norm_g[l:l + 1], l, B, S)
        o_mla = _attn(qf, kn, kpe, v, B, S)
        xf = _outproj(xf, mod_l, o_hg, o_mla, w_out_a[l], w_out_b[l], S)
        xf = _ffn(xf, mod_l, norm_g[l, 2:3], wg, wu, wd, fin_g, l, 1, 2, S, l == L - 1)
    return xf.reshape(B, S, D)
```

```python
import functools

import jax
import jax.numpy as jnp
from jax import lax
from jax.experimental import pallas as pl
from jax.experimental.pallas import tpu as pltpu

EPS = 1e-6
N_SUB = 3
HG_HEADS = 8
HG_DIM = 128
HG_CHUNK = 32
HG_GROUP = 4
HG_HEADS_PER_STEP = 4
HG_ROW_BLOCK = 512
MLA_HEADS = 8
MLA_NOPE = 128
MLA_ROPE = 64
MLA_V = 128
MLA_Q_RANK = 512
MLA_KV_RANK = 256
ROPE_THETA = 10000.0
MLA_QK_PAD = 256
MLA_HEADS_PER_STEP = 2
FFN_DOWN_SLAB = 512
FFN_TM = 1024

V7X_LANES = 128
V7X_MXU_DIM = 256
V7X_VMEM_BYTES = 64 * 1024 * 1024
VMEM_LIMIT = V7X_VMEM_BYTES - 1 * 1024 * 1024

F32 = jnp.float32
BF16 = jnp.bfloat16


def _params(*sem):
    return pltpu.CompilerParams(dimension_semantics=sem, vmem_limit_bytes=VMEM_LIMIT)


def _sigmoid(x):
    return 1.0 / (1.0 + jnp.exp(-x))


def _silu(x):
    return x * _sigmoid(x)


def _rms(x, g):
    ms = jnp.mean(x * x, axis=-1, keepdims=True)
    return (x * lax.rsqrt(ms + EPS)) * g


def _modulated_norm(x, g, shift, scale):
    return _rms(x, g) * (1.0 + scale) + shift


def _dot(a, b):
    return jnp.dot(a, b, preferred_element_type=F32)


def _dot_nt(a, b):
    return lax.dot_general(a, b, (((1,), (1,)), ((), ())), preferred_element_type=F32)


def _dot_tn(a, b):
    return lax.dot_general(a, b, (((0,), (0,)), ((), ())), preferred_element_type=F32)


def _tile(n, pref):
    t = min(n, pref)
    assert n % t == 0, (n, t)
    return t


def _ada_kernel(c_ref, w_ref, b_ref, o_ref):
    ca = _silu(c_ref[...]).astype(BF16)
    o_ref[...] = _dot(ca, w_ref[...].astype(BF16)) + b_ref[...]


def _ada(c, w_ada, b_ada):
    L, D, N = w_ada.shape
    B = c.shape[0]
    tn = _tile(N, 1024)
    return pl.pallas_call(
        _ada_kernel,
        grid=(L, N // tn),
        in_specs=[
            pl.BlockSpec((B, D), lambda l, j: (0, 0)),
            pl.BlockSpec((None, D, tn), lambda l, j: (l, 0, j)),
            pl.BlockSpec((None, 1, tn), lambda l, j: (l, 0, j)),
        ],
        out_specs=pl.BlockSpec((None, B, tn), lambda l, j: (l, 0, j)),
        out_shape=jax.ShapeDtypeStruct((L, B, N), F32),
        compiler_params=_params("parallel", "parallel"),
    )(c, w_ada, b_ada.reshape(L, 1, N))


def _ffn_kernel(x_hbm, mod_ref, modn_ref, g_ref, wg_ref, wu_ref, wd_ref, fin_ref, o_ref,
                xbuf, h_cur, h_next, sem, *, sub, tm, rows, n_slices, final):
    i = pl.program_id(0)
    j = pl.program_id(1)
    has_next = i + 1 < pl.num_programs(0)
    shift, scale, gate_row = 3 * sub, 3 * sub + 1, 3 * sub + 2

    def fetch(tile):
        return pltpu.make_async_copy(x_hbm.at[pl.ds(tile * tm, tm), :], xbuf.at[pl.ds(0, tm), :], sem)

    def norm_rows(r0, nrows, mod_src):
        h = _modulated_norm(xbuf[pl.ds(r0, nrows), :], g_ref[...],
                            mod_src[shift:shift + 1, :], mod_src[scale:scale + 1, :])
        h_next[pl.ds(r0, nrows), :] = h.astype(BF16)
        return h

    @pl.when((i == 0) & (j == 0))
    def _():
        first = fetch(0)
        first.start()
        xbuf[pl.ds(tm, rows), :] = jnp.zeros((rows, xbuf.shape[1]), F32)
        first.wait()
        for s in range(n_slices):
            norm_rows(s * rows, rows, mod_ref)

    @pl.when(j == 0)
    def _():
        h_cur[...] = h_next[pl.ds(0, tm), :]

    @pl.when((j == 2) & has_next)
    def _():
        fetch(i + 1).wait()

    d_model = o_ref.shape[1]
    tf = wg_ref.shape[1]
    slab = min(d_model, FFN_DOWN_SLAB)

    def matmul_step(first):
        zero = None
        if not first:
            scheduled = (j >= 2) & (j < 2 + n_slices)
            r0 = pl.multiple_of(jnp.where(scheduled, (j - 2) * rows, tm), rows)
            hn = norm_rows(r0, rows, modn_ref)
            bits = pltpu.bitcast(hn, jnp.uint32).reshape(rows // 8, 8, d_model)
            folded = bits[0]
            for r in range(1, rows // 8):
                folded = folded | bits[r]
            lanes = folded[:, 0:tf]
            for k in range(1, d_model // tf):
                lanes = lanes | folded[:, k * tf:(k + 1) * tf]
            half_word = jnp.uint32(16)
            zero = pltpu.bitcast(
                lax.shift_right_logical(lax.shift_right_logical(lanes, half_word), half_word), F32)

        h = h_cur[...]
        gate = _dot(h, wg_ref[...])
        if zero is not None:
            gate = (gate.reshape(tm // 8, 8, tf) + zero).reshape(tm, tf)
        up = _dot(h, wu_ref[...].astype(BF16))
        a = (_silu(gate) * up).astype(BF16)
        c = 0.5 * mod_ref[gate_row:gate_row + 1, :]
        for n in range(d_model // slab):
            sl = slice(n * slab, (n + 1) * slab)
            y = c[:, sl] * _dot(a, wd_ref[:, sl].astype(BF16))
            base = xbuf[pl.ds(0, tm), sl] if first else o_ref[:, sl]
            o_ref[:, sl] = base + y

    pl.when(j == 0)(functools.partial(matmul_step, True))
    pl.when(j > 0)(functools.partial(matmul_step, False))

    @pl.when((j == 0) & has_next)
    def _():
        fetch(i + 1).start()

    if final:
        @pl.when(j == pl.num_programs(1) - 1)
        def _():
            o_ref[...] = _rms(o_ref[...], fin_ref[...])


def _ffn(x, mod_l, g, wg, wu, wd, fin_g, layer, which, sub, seq, final):
    T, D = x.shape
    F = wg.shape[-1]
    tm = _tile(seq, FFN_TM)
    tf = _tile(F, 512)
    nt, nf = T // tm, F // tf
    per_seq = seq // tm
    n_slices = 1
    while 2 * n_slices <= min(nf - 2, 8):
        n_slices *= 2
    assert nf >= 2 + n_slices and tm % (16 * n_slices) == 0, (nf, tm)
    rows = tm // n_slices
    mod_spec = lambda tile_of: pl.BlockSpec((None, 3 * N_SUB, D), lambda i, j: (tile_of(i) // per_seq, 0, 0))
    return pl.pallas_call(
        functools.partial(_ffn_kernel, sub=sub, tm=tm, rows=rows, n_slices=n_slices, final=final),
        grid=(nt, nf),
        in_specs=[
            pl.BlockSpec(memory_space=pl.ANY),
            mod_spec(lambda i: i),
            mod_spec(lambda i: jnp.minimum(i + 1, nt - 1)),
            pl.BlockSpec((1, D), lambda i, j: (0, 0)),
            pl.BlockSpec((None, None, D, tf), lambda i, j: (layer, which, 0, j)),
            pl.BlockSpec((None, None, D, tf), lambda i, j: (layer, which, 0, j)),
            pl.BlockSpec((None, None, tf, D), lambda i, j: (layer, which, j, 0)),
            pl.BlockSpec((1, D), lambda i, j: (0, 0)),
        ],
        out_specs=pl.BlockSpec((tm, D), lambda i, j: (i, 0)),
        out_shape=jax.ShapeDtypeStruct((T, D), F32),
        scratch_shapes=[pltpu.VMEM((tm + rows, D), F32), pltpu.VMEM((tm, D), BF16),
                        pltpu.VMEM((tm + rows, D), BF16), pltpu.SemaphoreType.DMA(())],
        compiler_params=_params("arbitrary", "arbitrary"),
    )(x, mod_l, mod_l, g, wg, wu, wd, fin_g)


def _inproj_kernel(x_ref, mod_ref, g_ref, w_ref, tab_ref, qg_ref, kg_ref, wq_ref, wk_ref, wv_ref,
                   h_ref, qf_ref, kn_ref, kpe_ref, v_ref, *, sub, tm, rb):
    scale = (MLA_NOPE + MLA_ROPE) ** -0.5
    lane = lax.broadcasted_iota(jnp.int32, (rb, V7X_LANES), 1)
    low = lane < MLA_ROPE
    half = MLA_ROPE // 2
    P = MLA_QK_PAD

    for r in range(tm // rb):
        rows = slice(r * rb, (r + 1) * rb)
        tab = tab_ref[rows, :]

        def rope(y2):
            y = y2 * tab
            return jnp.where(low, y + pltpu.roll(y, MLA_ROPE, axis=1), 0.0)

        h = _modulated_norm(x_ref[rows, :], g_ref[...], mod_ref[3 * sub:3 * sub + 1, :],
                            mod_ref[3 * sub + 1:3 * sub + 2, :]).astype(BF16)
        h_ref[rows, :] = h
        proj = _dot(h, w_ref[...])
        qa = proj[:, :MLA_Q_RANK]
        kva = proj[:, MLA_Q_RANK:MLA_Q_RANK + MLA_KV_RANK]
        pe = proj[:, MLA_Q_RANK + MLA_KV_RANK:MLA_Q_RANK + MLA_KV_RANK + V7X_LANES]

        cq = _rms(qa, qg_ref[...]).astype(BF16)
        qf = _dot(cq, wq_ref[...])
        ckv = _rms(kva, kg_ref[...]).astype(BF16)
        kn_ref[rows, :] = _dot(ckv, wk_ref[...]).astype(BF16)
        v_ref[rows, :] = _dot(ckv, wv_ref[...]).astype(BF16)
        pe2 = jnp.where(low, pe, jnp.where(lane < MLA_ROPE + half, -pltpu.roll(pe, half, axis=1),
                                           pltpu.roll(pe, V7X_LANES - half, axis=1)))
        kpe_ref[rows, :] = rope(pe2).astype(BF16)
        for hd in range(MLA_HEADS):
            qf_ref[rows, hd * P:hd * P + MLA_NOPE] = (qf[:, hd * P:hd * P + MLA_NOPE] * scale).astype(BF16)
            qf_ref[rows, hd * P + MLA_NOPE:(hd + 1) * P] = (
                rope(qf[:, hd * P + MLA_NOPE:(hd + 1) * P]) * scale).astype(BF16)


def _inproj(x, mod_l, g, w, tab, qg, kg, wq, wk, wv, layer, seq):
    T, D = x.shape
    n_hg = 4 * HG_HEADS * HG_DIM
    N = w.shape[-1] - n_hg
    assert n_hg % N == 0
    col_block = n_hg // N
    tm = _tile(seq, 1024)
    rb = _tile(tm, 512)
    per_seq = seq // tm
    H = MLA_HEADS
    row = lambda width: pl.BlockSpec((tm, width), lambda i: (i, 0))
    full = lambda a: pl.BlockSpec(a.shape, lambda i: (0, 0), pipeline_mode=pl.Buffered(1))
    widths = (D, H * MLA_QK_PAD, H * MLA_NOPE, V7X_LANES, H * MLA_V)
    return pl.pallas_call(
        functools.partial(_inproj_kernel, sub=1, tm=tm, rb=rb),
        grid=(T // tm,),
        in_specs=[
            row(D),
            pl.BlockSpec((None, 3 * N_SUB, D), lambda i: (i // per_seq, 0, 0)),
            pl.BlockSpec((1, D), lambda i: (0, 0)),
            pl.BlockSpec((None, D, N), lambda i: (layer, 0, col_block), pipeline_mode=pl.Buffered(1)),
            row(V7X_LANES), full(qg), full(kg), full(wq), full(wk), full(wv),
        ],
        out_specs=[row(wd) for wd in widths],
        out_shape=[jax.ShapeDtypeStruct((T, wd), BF16) for wd in widths],
        compiler_params=_params("parallel"),
    )(x, mod_l, g, w, tab, qg, kg, wq, wk, wv)


def _hgrn_head(hq, z, hi, hg, lb, gn, st, *, ts):
    C = HG_CHUNK
    nc = ts // C
    u = jnp.exp(-jnp.abs(z))
    r = 1.0 / (1.0 + u)
    ur = u * r
    z_pos = z >= 0.0
    log_f = jnp.log(lb + (1.0 - lb) * jnp.where(z_pos, r, ur))
    k = (1.0 - lb) * jnp.where(z_pos, ur, r)
    q = _silu(hq)

    rows8 = 8
    per_chunk = C // rows8
    x = log_f.reshape(ts // rows8, rows8, HG_DIM)
    sub = lax.broadcasted_iota(jnp.int32, (1, rows8, HG_DIM), 1)
    step = 1
    while step < rows8:
        x = x + jnp.where(sub >= step, pltpu.roll(x, step, axis=1), 0.0)
        step *= 2
    x = x.reshape(nc, per_chunk, rows8, HG_DIM)
    parts = [x[:, 0]]
    carry = x[:, 0, rows8 - 1:rows8, :]
    for i in range(1, per_chunk):
        parts.append(x[:, i] + carry)
        if i + 1 < per_chunk:
            carry = carry + x[:, i, rows8 - 1:rows8, :]
    b3 = jnp.concatenate(parts, axis=1)

    b_end = b3[:, C - 1:C, :]
    q3 = q.reshape(nc, C, HG_DIM)
    k3 = k.reshape(nc, C, HG_DIM)
    decay = jnp.exp(b_end)
    q_in = (q3 * jnp.exp(b3)).astype(BF16).reshape(ts, HG_DIM)
    k_back = k3 * jnp.exp(-b3)
    k_in = k_back.astype(BF16).reshape(ts, HG_DIM)
    k_out = (k_back * decay).astype(BF16).reshape(ts, HG_DIM)
    v = hi.astype(BF16)

    G = HG_GROUP
    R = G * C
    ri = lax.broadcasted_iota(jnp.int32, (R, R), 0)
    ci = lax.broadcasted_iota(jnp.int32, (R, R), 1)
    causal = (ci <= ri) & ((ri // C) == (ci // C))
    zeros = jnp.zeros((C, HG_DIM), BF16)

    def block_arranged(m):
        return jnp.concatenate(
            [jnp.concatenate([m[c * C:(c + 1) * C] if cc == c else zeros for cc in range(G)], axis=1)
             for c in range(G)], axis=0)

    outs = []
    for g in range(ts // R):
        sl = slice(g * R, (g + 1) * R)
        a = jnp.where(causal, _dot_nt(q_in[sl], k_in[sl]), 0.0)
        o = _dot(a.astype(BF16), v[sl])
        kv = _dot_tn(v[sl], block_arranged(k_out[sl]))
        starts = []
        for c in range(G):
            starts.append(st)
            st = st * decay[g * G + c] + kv[:, c * HG_DIM:(c + 1) * HG_DIM]
        st_all = jnp.concatenate(starts, axis=1).astype(BF16)
        outs.append(o + _dot_nt(block_arranged(q_in[sl]), st_all))
    o = jnp.concatenate(outs, axis=0)
    return (_rms(o, gn) * _silu(hg)).astype(BF16), st


def _hgrn_kernel(h_ref, wq_ref, wf_ref, wi_ref, wg_ref, lb_ref, gn_ref, o_ref, st_ref, *, ts, heads):
    @pl.when(pl.program_id(2) == 0)
    def _():
        st_ref[...] = jnp.zeros_like(st_ref)

    rb = min(ts, HG_ROW_BLOCK)
    states = [st_ref[h] for h in range(heads)]
    for r in range(ts // rb):
        rows = slice(r * rb, (r + 1) * rb)
        hb = h_ref[rows, :]
        ys = [_dot(hb, w[...]) for w in (wq_ref, wf_ref, wi_ref, wg_ref)]
        for h in range(heads):
            sl = slice(h * HG_DIM, (h + 1) * HG_DIM)
            hq, z, hi, hg = (y[:, sl] for y in ys)
            o, states[h] = _hgrn_head(hq, z, hi, hg, lb_ref[:, sl], gn_ref[...], states[h], ts=rb)
            o_ref[rows, sl] = o
    for h in range(heads):
        st_ref[h] = states[h]


def _hgrn(h, w, lb, gn, layer, batch, seq):
    T, D = h.shape
    ts = _tile(seq, 2048)
    ns = seq // ts
    hp = HG_HEADS_PER_STEP
    nh = HG_HEADS // hp
    wo = hp * HG_DIM

    def wcol(group):
        return pl.BlockSpec((None, D, wo), lambda b, p, s: (layer, 0, group * nh + p))

    return pl.pallas_call(
        functools.partial(_hgrn_kernel, ts=ts, heads=hp),
        grid=(batch, nh, ns),
        in_specs=[pl.BlockSpec((ts, D), lambda b, p, s: (b * ns + s, 0)),
                  wcol(0), wcol(1), wcol(2), wcol(3),
                  pl.BlockSpec((1, wo), lambda b, p, s: (0, p)),
                  pl.BlockSpec((1, HG_DIM), lambda b, p, s: (0, 0))],
        out_specs=pl.BlockSpec((ts, wo), lambda b, p, s: (b * ns + s, p)),
        out_shape=jax.ShapeDtypeStruct((T, HG_HEADS * HG_DIM), BF16),
        scratch_shapes=[pltpu.VMEM((hp, HG_DIM, HG_DIM), F32)],
        compiler_params=_params("parallel", "arbitrary", "arbitrary"),
    )(h, w, w, w, w, lb, gn)


def _attn_kernel(q_ref, kn_ref, kpe_ref, v_ref, o_ref, *, seq, t, heads):
    n = seq // t
    kpe = kpe_ref[...]
    k_all = [jnp.concatenate([kn_ref[:, hd * MLA_NOPE:(hd + 1) * MLA_NOPE], kpe], axis=1)
             for hd in range(heads)]
    keep = (lax.broadcasted_iota(jnp.int32, (t, t), 1)
            <= lax.broadcasted_iota(jnp.int32, (t, t), 0))

    def update(state, s, v):
        row_max = jnp.max(s, axis=-1, keepdims=True)
        if state is None:
            p = jnp.exp(s - row_max)
            return row_max, jnp.sum(p, axis=-1, keepdims=True), _dot(p.astype(BF16), v)
        m, l, acc = state
        m_new = jnp.maximum(m, row_max)
        alpha = jnp.exp(m - m_new)
        p = jnp.exp(s - m_new)
        return (m_new, alpha * l + jnp.sum(p, axis=-1, keepdims=True),
                alpha * acc + _dot(p.astype(BF16), v))

    for i in range(n):
        for hd in range(heads):
            q = q_ref[i * t:(i + 1) * t, hd * MLA_QK_PAD:(hd + 1) * MLA_QK_PAD]
            state = None
            for j in range(i + 1):
                s = _dot_nt(q, k_all[hd][j * t:(j + 1) * t, :])
                if j == i:
                    s = jnp.where(keep, s, -jnp.inf)
                state = update(state, s, v_ref[j * t:(j + 1) * t, hd * MLA_V:(hd + 1) * MLA_V])
            _, l, acc = state
            o_ref[i * t:(i + 1) * t, hd * MLA_V:(hd + 1) * MLA_V] = (acc / l).astype(BF16)


def _attn(qf, kn, kpe, v, batch, seq):
    T = qf.shape[0]
    t = _tile(seq, 512)
    H = MLA_HEADS
    hp = MLA_HEADS_PER_STEP
    return pl.pallas_call(
        functools.partial(_attn_kernel, seq=seq, t=t, heads=hp),
        grid=(batch, H // hp),
        in_specs=[
            pl.BlockSpec((seq, hp * MLA_QK_PAD), lambda b, h: (b, h)),
            pl.BlockSpec((seq, hp * MLA_NOPE), lambda b, h: (b, h)),
            pl.BlockSpec((seq, V7X_LANES), lambda b, h: (b, 0)),
            pl.BlockSpec((seq, hp * MLA_V), lambda b, h: (b, h)),
        ],
        out_specs=pl.BlockSpec((seq, hp * MLA_V), lambda b, h: (b, h)),
        out_shape=jax.ShapeDtypeStruct((T, H * MLA_V), BF16),
        compiler_params=_params("parallel", "parallel"),
    )(qf, kn, kpe, v)


def _outproj_kernel(x_ref, mod_ref, a_ref, b_ref, wa_ref, wb_ref, o_ref, *, sub):
    y = _dot(a_ref[...], wa_ref[...]) + _dot(b_ref[...], wb_ref[...])
    o_ref[...] = x_ref[...] + mod_ref[3 * sub + 2:3 * sub + 3, :] * y


def _outproj(x, mod_l, a, b, wa, wb, seq):
    T, D = x.shape
    tm = _tile(seq, 512)
    per_seq = seq // tm
    return pl.pallas_call(
        functools.partial(_outproj_kernel, sub=1),
        grid=(T // tm,),
        in_specs=[
            pl.BlockSpec((tm, D), lambda i: (i, 0)),
            pl.BlockSpec((None, 3 * N_SUB, D), lambda i: (i // per_seq, 0, 0)),
            pl.BlockSpec((tm, a.shape[1]), lambda i: (i, 0)),
            pl.BlockSpec((tm, b.shape[1]), lambda i: (i, 0)),
            pl.BlockSpec(wa.shape, lambda i: (0, 0)),
            pl.BlockSpec(wb.shape, lambda i: (0, 0)),
        ],
        out_specs=pl.BlockSpec((tm, D), lambda i: (i, 0)),
        out_shape=jax.ShapeDtypeStruct((T, D), F32),
        compiler_params=_params("parallel"),
    )(x, mod_l, a, b, wa, wb)


def _rotate_half_cols(w):
    half = MLA_ROPE // 2
    w1, w2 = w[..., :half], w[..., half:]
    return jnp.concatenate([-w2, w1], axis=-1)


def _prep_w_in(w_in):
    pad = (-w_in.shape[-1]) % (4 * V7X_MXU_DIM)
    return jnp.pad(w_in.astype(BF16), ((0, 0), (0, 0), (0, pad)))


def _prep_w_q(w_q_up):
    L, R, _ = w_q_up.shape
    w = w_q_up.reshape(L, R, MLA_HEADS, MLA_NOPE + MLA_ROPE)
    pe = w[..., MLA_NOPE:]
    w = jnp.concatenate([w, _rotate_half_cols(pe)], axis=-1)
    return w.reshape(L, R, MLA_HEADS * MLA_QK_PAD).astype(BF16)


def _prep_w_kv(w_kv_up):
    L, R, _ = w_kv_up.shape
    w = w_kv_up.reshape(L, R, MLA_HEADS, MLA_NOPE + MLA_V)
    wk = w[..., :MLA_NOPE].reshape(L, R, MLA_HEADS * MLA_NOPE)
    wv = w[..., MLA_NOPE:].reshape(L, R, MLA_HEADS * MLA_V)
    return wk.astype(BF16), wv.astype(BF16)


def kernel(x, c, positions, w_ada, b_ada, norm_g, w_in, qa_norm_g, w_q_up, kva_norm_g, w_kv_up,
           hg_lb_logits, hg_norm_g, w_out, ffn_w_gate, ffn_w_up, ffn_w_down, final_norm_g):
    B, S, D = x.shape
    L = w_ada.shape[0]
    T = B * S

    half = MLA_ROPE // 2
    inv_freq = ROPE_THETA ** (-jnp.arange(half, dtype=F32) / half)
    ang = positions.astype(F32).reshape(T, 1) * inv_freq
    cos, sin = jnp.cos(ang), jnp.sin(ang)
    tab = jnp.concatenate([cos, cos, sin, sin], axis=-1)

    lb_all = jnp.cumsum(jax.nn.softmax(hg_lb_logits.astype(F32), axis=0), axis=0)
    lb_all = lb_all - lb_all[0:1]

    w_in_p = _prep_w_in(w_in)
    w_q_p = _prep_w_q(w_q_up)
    w_k_p, w_v_p = _prep_w_kv(w_kv_up)
    hg_w = HG_HEADS * HG_DIM
    w_out_a = w_out[:, :hg_w, :].astype(BF16)
    w_out_b = w_out[:, hg_w:, :].astype(BF16)
    wg, wu, wd = ffn_w_gate.astype(BF16), ffn_w_up, ffn_w_down

    mod = _ada(c, w_ada, b_ada).reshape(L, B, 3 * N_SUB, D)

    fin_g = final_norm_g.reshape(1, D)
    xf = x.reshape(T, D)
    for l in range(L):
        mod_l = mod[l]
        xf = _ffn(xf, mod_l, norm_g[l, 0:1], wg, wu, wd, fin_g, l, 0, 0, S, False)
        h, qf, kn, kpe, v = _inproj(xf, mod_l, norm_g[l, 1:2], w_in_p, tab, qa_norm_g[l:l + 1],
                                    kva_norm_g[l:l + 1], w_q_p[l], w_k_p[l], w_v_p[l], l, S)
        o_hg = _hgrn(h, w_in_p, lb_all[l:l + 1], hg_norm_g[l:l + 1], l, B, S)
        o_mla = _attn(qf, kn, kpe, v, B, S)
        xf = _outproj(xf, mod_l, o_hg, o_mla, w_out_a[l], w_out_b[l], S)
        xf = _ffn(xf, mod_l, norm_g[l, 2:3], wg, wu, wd, fin_g, l, 1, 2, S, l == L - 1)
    return xf.reshape(B, S, D)
```
